```python
import math
import jax, jax.numpy as jnp
from jax import lax
import numpy as np

D_MODEL = 1024
BATCH = 4
SEQ = 8192
DEPTH = 2

MEM_TOKENS = 256
DN_HEADS = 4
DN_HEAD_DIM = 128
DN_WIDTH = DN_HEADS * DN_HEAD_DIM
DN_CONV = 4
DN_CHUNK = 64
SWA_HEADS = 4
SWA_HEAD_DIM = 64
SWA_WIDTH = SWA_HEADS * SWA_HEAD_DIM
SWA_PATTERNS = ((128, 1), (512, 4), (2048, 16))
SWA_BLOCK = 128
SG_GROUPS = 4
SG_GROUP_DIM = 64
SG_WIDTH = SG_GROUPS * SG_GROUP_DIM
SG_CHUNK = 128
IN_SIZES = (3 * DN_WIDTH, DN_WIDTH, DN_HEADS, DN_HEADS, 3 * SWA_WIDTH, 2 * SG_WIDTH)
IN_COLS = 3 * DN_WIDTH + DN_WIDTH + 2 * DN_HEADS + 3 * SWA_WIDTH + 2 * SG_WIDTH
MIX_WIDTH = DN_WIDTH + SWA_WIDTH + SG_WIDTH
XA_HEADS = 4
XA_HEAD_DIM = D_MODEL // XA_HEADS
D_FF = 2816
NORM_EPS = 1e-6

kernel_name = 'hymba_style_delta_dilated_gmlp_hybrid'


def rms_norm(x, gain):
    xf = x.astype(jnp.float32)
    y = xf * lax.rsqrt(jnp.mean(xf * xf, axis=-1, keepdims=True) + NORM_EPS)
    return (y * gain.astype(jnp.float32)).astype(x.dtype)


def layer_norm(x, gain, bias):
    xf = x.astype(jnp.float32)
    mu = jnp.mean(xf, axis=-1, keepdims=True)
    var = jnp.mean(jnp.square(xf - mu), axis=-1, keepdims=True)
    y = (xf - mu) * lax.rsqrt(var + NORM_EPS) * gain.astype(jnp.float32) + bias.astype(jnp.float32)
    return y.astype(x.dtype)


def swiglu(x, w_gate_up, w_down):
    gate, up = jnp.split(x @ w_gate_up, 2, axis=-1)
    return (jax.nn.silu(gate) * up) @ w_down


def alibi_slopes(n):
    return jnp.asarray([2.0 ** (-8.0 * (i + 1) / n) for i in range(n)], dtype=jnp.float32)


def causal_depthwise_conv(x, w):
    c = x.shape[-1]
    return lax.conv_general_dilated(
        x, w[:, None, :], window_strides=(1,), padding=[(w.shape[0] - 1, 0)],
        dimension_numbers=('NWC', 'WIO', 'NWC'), feature_group_count=c)


def l2_normalize(x):
    return x * lax.rsqrt(jnp.sum(x * x, axis=-1, keepdims=True) + NORM_EPS)


def chunk_gated_delta_rule(q, k, v, g, beta):
    B, S, H, Dk = q.shape
    Dv = v.shape[-1]
    C = DN_CHUNK
    N = S // C

    def to_chunks(t):
        return t.reshape(B, N, C, H, -1).transpose(0, 3, 1, 2, 4)

    q, k, v = to_chunks(q), to_chunks(k), to_chunks(v)
    g = g.reshape(B, N, C, H).transpose(0, 3, 1, 2)
    beta = beta.reshape(B, N, C, H).transpose(0, 3, 1, 2)
    gc = jnp.cumsum(g, axis=-1)
    tri_incl = jnp.tril(jnp.ones((C, C), dtype=bool))
    tri_strict = jnp.tril(jnp.ones((C, C), dtype=bool), -1)
    diff = gc[..., :, None] - gc[..., None, :]
    decay = jnp.where(tri_incl, jnp.exp(jnp.where(tri_incl, diff, 0.0)), 0.0)
    kb = k * beta[..., None]
    lower = jnp.where(tri_strict, jnp.einsum('bhncd,bhnsd->bhncs', kb, k) * decay, 0.0)
    eye = jnp.eye(C, dtype=q.dtype)
    t_inv = lax.linalg.triangular_solve(eye + lower, jnp.broadcast_to(eye, lower.shape),
                                        left_side=True, lower=True, unit_diagonal=True)
    u = jnp.einsum('bhncs,bhnse->bhnce', t_inv, v * beta[..., None])
    w = jnp.einsum('bhncs,bhnsd->bhncd', t_inv, kb * jnp.exp(gc)[..., None])
    a_qk = jnp.where(tri_incl, jnp.einsum('bhncd,bhnsd->bhncs', q, k) * decay, 0.0)
    g_last = gc[..., -1]
    k_tail = k * jnp.exp(g_last[..., None] - gc)[..., None]
    q_dec = q * jnp.exp(gc)[..., None]

    xs = tuple(jnp.moveaxis(t, 2, 0) for t in (q_dec, k_tail, u, w, a_qk, g_last))

    def step(state, inp):
        qd, kt, uc, wc, aqk, gl = inp
        v_new = uc - jnp.einsum('bhcd,bhde->bhce', wc, state)
        o = jnp.einsum('bhcd,bhde->bhce', qd, state) + jnp.einsum('bhcs,bhse->bhce', aqk, v_new)
        state = state * jnp.exp(gl)[..., None, None] + jnp.einsum('bhcd,bhce->bhde', kt, v_new)
        return state, o

    state0 = jnp.zeros((B, H, Dk, Dv), dtype=q.dtype)
    _, o = lax.scan(step, state0, xs)
    return o.transpose(1, 0, 3, 2, 4).reshape(B, S, H, Dv)


def gated_deltanet(qkv, z, b_raw, a_raw, conv_w, a_log, dt_bias, out_gain):
    B, S, _ = qkv.shape
    f32 = jnp.float32
    qkv = jax.nn.silu(causal_depthwise_conv(qkv.astype(f32), conv_w.astype(f32)))
    q, k, v = jnp.split(qkv, 3, axis=-1)
    q = l2_normalize(q.reshape(B, S, DN_HEADS, DN_HEAD_DIM)) * (DN_HEAD_DIM ** -0.5)
    k = l2_normalize(k.reshape(B, S, DN_HEADS, DN_HEAD_DIM))
    v = v.reshape(B, S, DN_HEADS, DN_HEAD_DIM)
    beta = jax.nn.sigmoid(b_raw.astype(f32))
    g = -jnp.exp(a_log.astype(f32)) * jax.nn.softplus(a_raw.astype(f32) + dt_bias.astype(f32))
    o = chunk_gated_delta_rule(q, k, v, g, beta)
    o = o * lax.rsqrt(jnp.mean(o * o, axis=-1, keepdims=True) + NORM_EPS) * out_gain.astype(f32)
    o = o * jax.nn.silu(z.astype(f32).reshape(B, S, DN_HEADS, DN_HEAD_DIM))
    return o.reshape(B, S, DN_WIDTH)


def dilated_window_attention(q, k, v, slopes, window, dilation):
    B, S, H, Dh = q.shape
    f32 = jnp.float32
    span = window // dilation
    L = S // dilation
    nb = -(-L // SWA_BLOCK)
    Lp = nb * SWA_BLOCK

    def to_sub(t):
        t = t.reshape(B, L, dilation, H, Dh).transpose(0, 2, 1, 3, 4)
        t = jnp.pad(t, ((0, 0), (0, 0), (0, Lp - L), (0, 0), (0, 0)))
        return t.reshape(B, dilation, nb, SWA_BLOCK, H, Dh)

    def with_prev(t):
        prev = jnp.pad(t[:, :, :-1], ((0, 0), (0, 0), (1, 0), (0, 0), (0, 0), (0, 0)))
        return jnp.concatenate([prev, t], axis=3)

    qs = to_sub(q)
    kw = with_prev(to_sub(k))
    vw = with_prev(to_sub(v))
    qi = jnp.arange(SWA_BLOCK)[:, None]
    kj = jnp.arange(2 * SWA_BLOCK)[None, :]
    rel = SWA_BLOCK + qi - kj
    first = (jnp.arange(nb) == 0)[:, None, None] & (kj < SWA_BLOCK)[None]
    valid = ((rel >= 0) & (rel <= span))[None] & ~first
    bias = -slopes[:, None, None] * (rel * dilation).astype(f32)[None]
    s = jnp.einsum('bdnqhe,bdnkhe->bdnhqk', qs, kw).astype(f32) * (Dh ** -0.5) + bias
    s = jnp.where(valid[:, None], s, -jnp.inf)
    m = jnp.max(s, axis=-1, keepdims=True)
    p = jnp.exp(s - m)
    den = jnp.sum(p, axis=-1)
    o = jnp.einsum('bdnhqk,bdnkhe->bdnqhe', p, vw.astype(f32)) / jnp.swapaxes(den, -1, -2)[..., None]
    lse = jnp.swapaxes(m[..., 0] + jnp.log(den), -1, -2)

    def from_sub(t):
        t = t.reshape((B, dilation, Lp) + t.shape[4:])[:, :, :L]
        t = jnp.swapaxes(t, 1, 2)
        return t.reshape((B, S) + t.shape[3:])

    return from_sub(o), from_sub(lse)


def dilated_mixture_attention(qkv):
    B, S, _ = qkv.shape
    q, k, v = (t.reshape(B, S, SWA_HEADS, SWA_HEAD_DIM) for t in jnp.split(qkv, 3, axis=-1))
    slopes = alibi_slopes(SWA_HEADS)
    outs, lses = [], []
    for window, dilation in SWA_PATTERNS:
        o, lse = dilated_window_attention(q, k, v, slopes, window, dilation)
        outs.append(o)
        lses.append(lse)
    alpha = jax.nn.softmax(jnp.stack(lses), axis=0)
    o = jnp.einsum('pbsh,pbshe->bshe', alpha, jnp.stack(outs))
    return o.reshape(B, S, SWA_WIDTH)


def chunked_spatial_gating(uv, ln_gain, ln_bias, w_spatial, b_spatial):
    B, S, _ = uv.shape
    n_chunks = S // SG_CHUNK
    u, v = jnp.split(jax.nn.gelu(uv), 2, axis=-1)
    v = layer_norm(v, ln_gain, ln_bias).reshape(B, n_chunks, SG_CHUNK, SG_GROUPS, SG_GROUP_DIM)
    w_causal = jnp.tril(w_spatial)
    mixed = jnp.einsum('gts,bnsgd->bntgd', w_causal, v) + b_spatial.T[:, :, None]
    gated = u.reshape(B, n_chunks, SG_CHUNK, SG_GROUPS, SG_GROUP_DIM) * mixed
    return gated.reshape(B, S, SG_WIDTH)


def parallel_mixing(n, w_in, conv_w, a_log, dt_bias, dn_gain, sg_gain, sg_bias, w_sp, b_sp, w_out):
    proj = n @ w_in
    cuts = []
    acc = 0
    for size in IN_SIZES[:-1]:
        acc += size
        cuts.append(acc)
    dn_qkv, dn_z, dn_b, dn_a, swa_qkv, sg_uv = jnp.split(proj, cuts, axis=-1)
    out_a = gated_deltanet(dn_qkv, dn_z, dn_b, dn_a, conv_w, a_log, dt_bias, dn_gain)
    out_b = dilated_mixture_attention(swa_qkv)
    out_c = chunked_spatial_gating(sg_uv, sg_gain, sg_bias, w_sp, b_sp)
    merged = jnp.concatenate([out_a.astype(n.dtype), out_b.astype(n.dtype), out_c.astype(n.dtype)], axis=-1)
    return merged @ w_out


def memory_cross_attention(h, mem, w_q, w_kv, w_o):
    B, S, _ = h.shape
    M = mem.shape[1]
    q = (h @ w_q).reshape(B, S, XA_HEADS, XA_HEAD_DIM)
    k, v = jnp.split(mem @ w_kv, 2, axis=-1)
    k = k.reshape(B, M, XA_HEADS, XA_HEAD_DIM)
    v = v.reshape(B, M, XA_HEADS, XA_HEAD_DIM)
    s = jnp.einsum('bshe,bmhe->bhsm', q, k).astype(jnp.float32) * (XA_HEAD_DIM ** -0.5)
    p = jax.nn.softmax(s, axis=-1)
    o = jnp.einsum('bhsm,bmhe->bshe', p.astype(v.dtype), v).reshape(B, S, D_MODEL)
    return o @ w_o


def setup_inputs(seed: int = 0) -> dict:
    key = jax.random.key(seed)
    ks = jax.random.split(key, 32)
    f32 = jnp.float32

    def nrm(k, shape, scale):
        return jax.random.normal(k, shape, f32) * scale

    def gain(k, shape):
        return 1.0 + 0.02 * jax.random.normal(k, shape, f32)

    dt = jnp.exp(jax.random.uniform(ks[9], (DEPTH, DN_HEADS), f32, math.log(1e-3), math.log(1e-1)))
    return {
        'x': nrm(ks[0], (BATCH, SEQ, D_MODEL), 1.0),
        'mem': nrm(ks[1], (BATCH, MEM_TOKENS, D_MODEL), 1.0),
        'ffn1_norm': gain(ks[2], (DEPTH, D_MODEL)),
        'ffn1_w_gate_up': nrm(ks[3], (DEPTH, D_MODEL, 2 * D_FF), D_MODEL ** -0.5),
        'ffn1_w_down': nrm(ks[4], (DEPTH, D_FF, D_MODEL), D_FF ** -0.5),
        'mix_norm': gain(ks[5], (DEPTH, D_MODEL)),
        'mix_w_in': nrm(ks[6], (DEPTH, D_MODEL, IN_COLS), D_MODEL ** -0.5),
        'dn_conv_w': nrm(ks[7], (DEPTH, DN_CONV, 3 * DN_WIDTH), DN_CONV ** -0.5),
        'dn_a_log': jnp.log(jax.random.uniform(ks[8], (DEPTH, DN_HEADS), f32, 1.0, 16.0)),
        'dn_dt_bias': dt + jnp.log(-jnp.expm1(-dt)),
        'dn_out_norm': gain(ks[10], (DEPTH, DN_HEAD_DIM)),
        'sg_norm_gain': gain(ks[11], (DEPTH, SG_WIDTH)),
        'sg_norm_bias': nrm(ks[12], (DEPTH, SG_WIDTH), 0.02),
        'sg_w_spatial': nrm(ks[13], (DEPTH, SG_GROUPS, SG_CHUNK, SG_CHUNK), SG_CHUNK ** -0.5),
        'sg_b_spatial': gain(ks[14], (DEPTH, SG_GROUPS, SG_CHUNK)),
        'mix_w_out': nrm(ks[15], (DEPTH, MIX_WIDTH, D_MODEL), MIX_WIDTH ** -0.5),
        'xa_norm': gain(ks[16], (DEPTH, D_MODEL)),
        'xa_mem_norm': gain(ks[17], (DEPTH, D_MODEL)),
        'xa_w_q': nrm(ks[18], (DEPTH, D_MODEL, D_MODEL), D_MODEL ** -0.5),
        'xa_w_kv': nrm(ks[19], (DEPTH, D_MODEL, 2 * D_MODEL), D_MODEL ** -0.5),
        'xa_w_o': nrm(ks[20], (DEPTH, D_MODEL, D_MODEL), D_MODEL ** -0.5),
        'ffn2_norm': gain(ks[21], (DEPTH, D_MODEL)),
        'ffn2_w_gate_up': nrm(ks[22], (DEPTH, D_MODEL, 2 * D_FF), D_MODEL ** -0.5),
        'ffn2_w_down': nrm(ks[23], (DEPTH, D_FF, D_MODEL), D_FF ** -0.5),
        'final_norm': gain(ks[24], (D_MODEL,)),
    }


def reference(x, mem, ffn1_norm, ffn1_w_gate_up, ffn1_w_down, mix_norm, mix_w_in, dn_conv_w,
              dn_a_log, dn_dt_bias, dn_out_norm, sg_norm_gain, sg_norm_bias, sg_w_spatial,
              sg_b_spatial, mix_w_out, xa_norm, xa_mem_norm, xa_w_q, xa_w_kv, xa_w_o,
              ffn2_norm, ffn2_w_gate_up, ffn2_w_down, final_norm):
    h = x
    for i in range(DEPTH):
        h = h + 0.5 * swiglu(rms_norm(h, ffn1_norm[i]), ffn1_w_gate_up[i], ffn1_w_down[i])
        h = h + parallel_mixing(rms_norm(h, mix_norm[i]), mix_w_in[i], dn_conv_w[i], dn_a_log[i],
                                dn_dt_bias[i], dn_out_norm[i], sg_norm_gain[i], sg_norm_bias[i],
                                sg_w_spatial[i], sg_b_spatial[i], mix_w_out[i])
        h = h + memory_cross_attention(rms_norm(h, xa_norm[i]), rms_norm(mem, xa_mem_norm[i]),
                                       xa_w_q[i], xa_w_kv[i], xa_w_o[i])
        h = h + 0.5 * swiglu(rms_norm(h, ffn2_norm[i]), ffn2_w_gate_up[i], ffn2_w_down[i])
    return rms_norm(h, final_norm)
```

```python
import functools

import jax
import jax.numpy as jnp
from jax import lax
from jax.experimental import pallas as pl
from jax.experimental.pallas import tpu as pltpu

F32 = jnp.float32
BF16 = jnp.bfloat16
NORM_EPS = 1e-6
LANES = 128
VMEM_LIMIT = 56 * 1024 * 1024

DN_HEADS = 4
DN_DIM = 128
DN_WIDTH = DN_HEADS * DN_DIM
DN_CONV = 4
DN_CHUNK = 64
SWA_HEADS = 4
SWA_DIM = 64
SWA_WIDTH = SWA_HEADS * SWA_DIM
SWA_PATTERNS = ((128, 1), (512, 4), (2048, 16))
SWA_BLOCK = 128
SWA_OUT = SWA_WIDTH + LANES
SG_GROUPS = 4
SG_DIM = 64
SG_WIDTH = SG_GROUPS * SG_DIM
SG_CHUNK = 128
XA_HEADS = 4
NEG_BIG = -1e30


def _resident(shape):
    return pl.BlockSpec(shape, lambda *_: (0,) * len(shape), pipeline_mode=pl.Buffered(1))


def _params(*sem):
    return pltpu.CompilerParams(dimension_semantics=sem, vmem_limit_bytes=VMEM_LIMIT)


def _rms(x, gain):
    ms = jnp.mean(x * x, axis=-1, keepdims=True)
    return x * lax.rsqrt(ms + NORM_EPS) * gain


def _dot(a, b):
    return jnp.dot(a, b, preferred_element_type=F32)


def _dot_nt(a, b):
    return lax.dot_general(a, b, (((1,), (1,)), ((), ())), preferred_element_type=F32)


def _dot_tn(a, b):
    return lax.dot_general(a, b, (((0,), (0,)), ((), ())), preferred_element_type=F32)


def _silu(x):
    return x * jax.nn.sigmoid(x)


def _ffn_kernel(*refs, d_ff, chunk, final):
    if final:
        x_ref, g_ref, wgu_ref, wd_ref, fg_ref, o_ref, h_scr = refs
    else:
        x_ref, g_ref, wgu_ref, wd_ref, o_ref, h_scr = refs
    x = x_ref[...]
    n = _rms(x, g_ref[...]).astype(BF16)
    for c in range(d_ff // chunk):
        gate = _dot(n, wgu_ref[:, c * chunk:(c + 1) * chunk])
        up = _dot(n, wgu_ref[:, d_ff + c * chunk:d_ff + (c + 1) * chunk])
        h_scr[:, c * chunk:(c + 1) * chunk] = (_silu(gate) * up).astype(BF16)
    y = x + 0.5 * _dot(h_scr[...], wd_ref[...])
    if final:
        y = _rms(y, fg_ref[...])
    o_ref[...] = y


def _ffn(h, gain, w_gu, w_d, final_gain=None, tm=512, chunk=256):
    t, d = h.shape
    d_ff = w_d.shape[0]
    final = final_gain is not None
    row = pl.BlockSpec((tm, d), lambda i: (i, 0))
    vec = pl.BlockSpec((1, d), lambda i: (0, 0))
    in_specs = [row, vec, _resident((d, 2 * d_ff)), _resident((d_ff, d))]
    args = [h, gain.reshape(1, d), w_gu, w_d]
    if final:
        in_specs.append(vec)
        args.append(final_gain.reshape(1, d))
    return pl.pallas_call(
        functools.partial(_ffn_kernel, d_ff=d_ff, chunk=chunk, final=final),
        grid=(t // tm,),
        in_specs=in_specs,
        out_specs=row,
        out_shape=jax.ShapeDtypeStruct((t, d), F32),
        scratch_shapes=[pltpu.VMEM((tm, d_ff), BF16)],
        compiler_params=_params("parallel"),
        name="ffn_final" if final else "ffn",
    )(*args)


def _proj_kernel(x_ref, g_ref, w_ref, lg_ref, lb_ref, wsp_ref, bsp_ref,
                 qkv_ref, z_ref, ba_ref, swa_ref, c_ref, *, tm):
    n = _rms(x_ref[...], g_ref[...]).astype(BF16)
    o_z = 3 * DN_WIDTH
    o_swa = o_z + DN_WIDTH
    o_sg = o_swa + 3 * SWA_WIDTH
    o_ba = o_sg + 2 * SG_WIDTH
    qkv_ref[...] = _dot(n, w_ref[:, 0:o_z])
    z_ref[...] = _dot(n, w_ref[:, o_z:o_swa])
    swa_ref[...] = _dot(n, w_ref[:, o_swa:o_sg]).astype(BF16)
    ba_ref[...] = _dot(n, w_ref[:, o_ba:o_ba + LANES])

    uv = jax.nn.gelu(_dot(n, w_ref[:, o_sg:o_ba]))
    u = uv[:, :SG_WIDTH]
    v = uv[:, SG_WIDTH:]
    mu = jnp.mean(v, axis=-1, keepdims=True)
    vc = v - mu
    var = jnp.mean(vc * vc, axis=-1, keepdims=True)
    vn = vc * lax.rsqrt(var + NORM_EPS) * lg_ref[...] + lb_ref[...]
    row = lax.broadcasted_iota(jnp.int32, (SG_CHUNK, SG_CHUNK), 0)
    col = lax.broadcasted_iota(jnp.int32, (SG_CHUNK, SG_CHUNK), 1)
    causal = row >= col
    low_lanes = col < SG_DIM
    w_causal = [jnp.where(causal, wsp_ref[g], 0.0).astype(BF16) for g in range(SG_GROUPS)]
    for c in range(tm // SG_CHUNK):
        rows = slice(c * SG_CHUNK, (c + 1) * SG_CHUNK)
        for p in range(SG_GROUPS // 2):
            lanes = slice(p * LANES, (p + 1) * LANES)
            vp = vn[rows, lanes]
            v_lo = jnp.where(low_lanes, vp, 0.0).astype(BF16)
            v_hi = jnp.where(low_lanes, 0.0, vp).astype(BF16)
            mixed = _dot(w_causal[2 * p], v_lo) + _dot(w_causal[2 * p + 1], v_hi) + bsp_ref[:, lanes]
            c_ref[rows, lanes] = (u[rows, lanes] * mixed).astype(BF16)


def _proj(h, gain, w_all, sg_gain, sg_bias, w_sp, b_sp_rows, tm=512):
    t, d = h.shape
    ncols = w_all.shape[1]

    def row(w):
        return pl.BlockSpec((tm, w), lambda i: (i, 0))

    return pl.pallas_call(
        functools.partial(_proj_kernel, tm=tm),
        grid=(t // tm,),
        in_specs=[row(d), pl.BlockSpec((1, d), lambda i: (0, 0)), _resident((d, ncols)),
                  pl.BlockSpec((1, SG_WIDTH), lambda i: (0, 0)),
                  pl.BlockSpec((1, SG_WIDTH), lambda i: (0, 0)),
                  _resident((SG_GROUPS, SG_CHUNK, SG_CHUNK)),
                  _resident((SG_CHUNK, SG_WIDTH))],
        out_specs=[row(3 * DN_WIDTH), row(DN_WIDTH), row(LANES), row(3 * SWA_WIDTH), row(SG_WIDTH)],
        out_shape=[jax.ShapeDtypeStruct((t, 3 * DN_WIDTH), F32),
                   jax.ShapeDtypeStruct((t, DN_WIDTH), F32),
                   jax.ShapeDtypeStruct((t, LANES), F32),
                   jax.ShapeDtypeStruct((t, 3 * SWA_WIDTH), BF16),
                   jax.ShapeDtypeStruct((t, SG_WIDTH), BF16)],
        compiler_params=_params("parallel"),
        name="mix_proj",
    )(h, gain.reshape(1, d), w_all, sg_gain.reshape(1, -1), sg_bias.reshape(1, -1), w_sp, b_sp_rows)


def _split_bf16(a):
    hi = a.astype(BF16)
    lo = (a - hi.astype(F32)).astype(BF16)
    return hi, lo


def _dot_split(a, b):
    ah, al = a
    bh, bl = b
    return _dot(ah, bh) + _dot(ah, bl) + _dot(al, bh)


def _unit_lower_inverse(lower, eye, base_mask, level_masks):
    t = eye - lower * base_mask
    for m in level_masks:
        ts = _split_bf16(t)
        t = t - _dot_split(_split_bf16(_dot_split(ts, _split_bf16(lower * m))), ts)
    return t


def _dn_kernel(qkv_ref, z_ref, ba_ref, cw_ref, gp_ref, og_ref, o_ref, xs_ref, st_ref, *, ts):
    c_len = DN_CHUNK
    halo = 8
    width = 3 * DN_WIDTH

    @pl.when(pl.program_id(1) == 0)
    def _():
        xs_ref[0:halo, :] = jnp.zeros((halo, width), F32)
        st_ref[...] = jnp.zeros_like(st_ref)

    xs_ref[halo:halo + ts, :] = qkv_ref[...]

    neg_a = -jnp.exp(gp_ref[0:1, :])
    dt_bias = gp_ref[1:2, :]
    out_gain = og_ref[...]
    cw = cw_ref[...]

    row = lax.broadcasted_iota(jnp.int32, (c_len, c_len), 0)
    col = lax.broadcasted_iota(jnp.int32, (c_len, c_len), 1)
    tri_incl = row >= col
    tri_strict = row > col
    eye = (row == col).astype(F32)
    tri_f = tri_incl.astype(F32)
    ones_f = jnp.ones((c_len, c_len), F32)
    base_mask = ((row >> 1) == (col >> 1)).astype(F32)
    level_masks = []
    sh = 1
    while (1 << sh) < c_len:
        level_masks.append((((row >> (sh + 1)) == (col >> (sh + 1))) & ((row >> sh) != (col >> sh))).astype(F32))
        sh += 1
    rowh = lax.broadcasted_iota(jnp.int32, (c_len, DN_HEADS * LANES), 0)
    colh = lax.broadcasted_iota(jnp.int32, (c_len, DN_HEADS * LANES), 1) & (LANES - 1)
    upto = (rowh <= colh).astype(F32)

    def chunk(c, carry):
        r0 = pl.multiple_of(c * c_len, c_len)
        x = xs_ref[pl.ds(r0, c_len + halo), :]
        acc = cw[DN_CONV - 1:DN_CONV, :] * x[halo:halo + c_len]
        for j in range(DN_CONV - 1):
            lo = halo - (DN_CONV - 1) + j
            acc = acc + cw[j:j + 1, :] * x[lo:lo + c_len]
        a = _silu(acc)

        ba = ba_ref[pl.ds(r0, c_len), :]
        beta_all = jax.nn.sigmoid(ba)
        g_all = neg_a * jax.nn.softplus(ba + dt_bias)
        gseg = jnp.concatenate(
            [jnp.broadcast_to(g_all[:, DN_HEADS + h:DN_HEADS + h + 1], (c_len, LANES)) for h in range(DN_HEADS)],
            axis=1)
        gcol = jnp.dot(tri_f, gseg, preferred_element_type=F32, precision=lax.Precision.HIGHEST)
        grow = jnp.dot(ones_f, gseg * upto, preferred_element_type=F32, precision=lax.Precision.HIGHEST)

        for h in range(DN_HEADS):
            lanes = slice(h * DN_DIM, (h + 1) * DN_DIM)
            q = a[:, h * DN_DIM:(h + 1) * DN_DIM]
            k = a[:, DN_WIDTH + h * DN_DIM:DN_WIDTH + (h + 1) * DN_DIM]
            v = a[:, 2 * DN_WIDTH + h * DN_DIM:2 * DN_WIDTH + (h + 1) * DN_DIM]
            q = q * lax.rsqrt(jnp.sum(q * q, axis=-1, keepdims=True) + NORM_EPS) * (DN_DIM ** -0.5)
            k = k * lax.rsqrt(jnp.sum(k * k, axis=-1, keepdims=True) + NORM_EPS)
            beta = beta_all[:, h:h + 1]
            gcb = gcol[:, lanes]
            grow_h = grow[:, lanes]
            g_last = grow_h[:, c_len:c_len + 1]
            diff = gcb[:, :c_len] - grow_h[:, :c_len]
            decay = jnp.where(tri_incl, jnp.exp(jnp.where(tri_incl, diff, 0.0)), 0.0)
            kb = k * beta
            k16 = k.astype(BF16)
            lower = jnp.where(tri_strict, _dot_nt(kb.astype(BF16), k16) * decay, 0.0)
            t_inv = _unit_lower_inverse(lower, eye, base_mask, level_masks).astype(BF16)
            egc = jnp.exp(gcb)
            u = _dot(t_inv, (v * beta).astype(BF16))
            w = _dot(t_inv, (kb * egc).astype(BF16))
            a_qk = jnp.where(tri_incl, _dot_nt(q.astype(BF16), k16) * decay, 0.0)
            k_tail = k * jnp.exp(g_last - gcb)
            q_dec = q * egc

            state = st_ref[h]
            s16 = state.astype(BF16)
            v_new = u - _dot(w.astype(BF16), s16)
            vn16 = v_new.astype(BF16)
            o = _dot(q_dec.astype(BF16), s16) + _dot(a_qk.astype(BF16), vn16)
            st_ref[h] = state * jnp.exp(g_last[0:1, :]) + _dot_tn(k_tail.astype(BF16), vn16)

            o = o * lax.rsqrt(jnp.mean(o * o, axis=-1, keepdims=True) + NORM_EPS) * out_gain
            zh = z_ref[pl.ds(r0, c_len), lanes]
            o_ref[pl.ds(r0, c_len), lanes] = (o * _silu(zh)).astype(o_ref.dtype)
        return carry

    lax.fori_loop(0, ts // c_len, chunk, 0)
    xs_ref[0:halo, :] = xs_ref[ts:ts + halo, :]


def _deltanet(qkv, z, ba, conv_w, gate_params, out_gain, batch, ts=512):
    t = qkv.shape[0]
    ns = t // batch // ts

    def row(w):
        return pl.BlockSpec((ts, w), lambda b, s: (b * ns + s, 0))

    return pl.pallas_call(
        functools.partial(_dn_kernel, ts=ts),
        grid=(batch, ns),
        in_specs=[row(3 * DN_WIDTH), row(DN_WIDTH), row(LANES),
                  pl.BlockSpec((DN_CONV, 3 * DN_WIDTH), lambda b, s: (0, 0)),
                  pl.BlockSpec((8, LANES), lambda b, s: (0, 0)),
                  pl.BlockSpec((1, DN_DIM), lambda b, s: (0, 0))],
        out_specs=row(DN_WIDTH),
        out_shape=jax.ShapeDtypeStruct((t, DN_WIDTH), BF16),
        scratch_shapes=[pltpu.VMEM((ts + 8, 3 * DN_WIDTH), F32),
                        pltpu.VMEM((DN_HEADS, DN_DIM, DN_DIM), F32)],
        compiler_params=_params("parallel", "arbitrary"),
        name="deltanet",
    )(qkv, z, ba, conv_w, gate_params, out_gain.reshape(1, DN_DIM))


def _swa_kernel(cur_ref, prev_ref, o_ref, *, lq, dilation):
    blk = SWA_BLOCK
    have_prev = pl.program_id(2) > 0
    row = lax.broadcasted_iota(jnp.int32, (blk, blk), 0)
    lane = lax.broadcasted_iota(jnp.int32, (blk, blk), 1)
    low = lane < SWA_DIM
    rel_cur = row - lane
    rel_prev = rel_cur + blk
    ok_cur = rel_cur >= 0
    ok_prev_static = rel_prev <= blk
    relf_cur = rel_cur.astype(F32)
    relf_prev = rel_prev.astype(F32)
    scale = SWA_DIM ** -0.5
    for i in range(lq // blk):
        rows = slice(i * blk, (i + 1) * blk)
        if i == 0:
            kv_prev = prev_ref
            prows = slice(0, blk)
            ok_prev = jnp.logical_and(ok_prev_static, have_prev)
        else:
            kv_prev = cur_ref
            prows = slice((i - 1) * blk, i * blk)
            ok_prev = ok_prev_static
        lse_tile = jnp.zeros((blk, LANES), F32)
        for p in range(SWA_HEADS // 2):
            ql = slice(p * LANES, (p + 1) * LANES)
            kl = slice(SWA_WIDTH + p * LANES, SWA_WIDTH + (p + 1) * LANES)
            vl = slice(2 * SWA_WIDTH + p * LANES, 2 * SWA_WIDTH + (p + 1) * LANES)
            qp = cur_ref[rows, ql]
            k_c = cur_ref[rows, kl]
            v_c = cur_ref[rows, vl]
            k_p = kv_prev[prows, kl]
            v_p = kv_prev[prows, vl]
            o_pair = jnp.zeros((blk, LANES), F32)
            for hh in range(2):
                h = 2 * p + hh
                mine = low if hh == 0 else jnp.logical_not(low)
                slope = (2.0 ** (-8.0 * (h + 1) / SWA_HEADS)) * dilation
                zero = jnp.zeros_like(qp)
                qh = jnp.where(mine, qp, zero)
                s_c = _dot_nt(qh, k_c) * scale - slope * relf_cur
                s_p = _dot_nt(qh, k_p) * scale - slope * relf_prev
                s_c = jnp.where(ok_cur, s_c, NEG_BIG)
                s_p = jnp.where(ok_prev, s_p, NEG_BIG)
                m = jnp.maximum(jnp.max(s_c, axis=-1, keepdims=True), jnp.max(s_p, axis=-1, keepdims=True))
                p_c = jnp.exp(s_c - m)
                p_p = jnp.exp(s_p - m)
                den = jnp.sum(p_c, axis=-1, keepdims=True) + jnp.sum(p_p, axis=-1, keepdims=True)
                pv = (_dot(p_c.astype(BF16), jnp.where(mine, v_c, zero))
                      + _dot(p_p.astype(BF16), jnp.where(mine, v_p, zero)))
                o_pair = o_pair + pv * (1.0 / den)
                lse_tile = jnp.where(lane == h, m + jnp.log(den), lse_tile)
            o_ref[rows, ql] = o_pair
        o_ref[rows, SWA_WIDTH:SWA_OUT] = lse_tile


def _swa_pattern(swa_qkv, batch, seq, dilation, lq=512):
    length = seq // dilation
    lq = min(lq, length)
    per = lq // SWA_BLOCK
    x = swa_qkv.reshape(batch, length, dilation * 3 * SWA_WIDTH)
    out = pl.pallas_call(
        functools.partial(_swa_kernel, lq=lq, dilation=dilation),
        grid=(batch, dilation, length // lq),
        in_specs=[pl.BlockSpec((None, lq, 3 * SWA_WIDTH), lambda b, r, j: (b, j, r)),
                  pl.BlockSpec((None, SWA_BLOCK, 3 * SWA_WIDTH),
                               lambda b, r, j: (b, jnp.maximum(j * per - 1, 0), r))],
        out_specs=pl.BlockSpec((None, lq, SWA_OUT), lambda b, r, j: (b, j, r)),
        out_shape=jax.ShapeDtypeStruct((batch, length, dilation * SWA_OUT), F32),
        compiler_params=_params("parallel", "parallel", "parallel"),
        name=f"swa_d{dilation}",
    )(x, x)
    return out.reshape(batch * seq, SWA_OUT)


def _mix_out_kernel(h_ref, a_ref, s1_ref, s2_ref, s3_ref, c_ref, w_ref, o_ref, *, tm):
    pats = (s1_ref, s2_ref, s3_ref)
    lses = [r[:, SWA_WIDTH:SWA_OUT] for r in pats]
    m = jnp.maximum(jnp.maximum(lses[0], lses[1]), lses[2])
    es = [jnp.exp(l - m) for l in lses]
    inv = 1.0 / (es[0] + es[1] + es[2])
    low = lax.broadcasted_iota(jnp.int32, (tm, LANES), 1) < SWA_DIM
    y = h_ref[...] + _dot(a_ref[...], w_ref[0:DN_WIDTH, :])
    for p in range(SWA_HEADS // 2):
        lanes = slice(p * LANES, (p + 1) * LANES)
        merged = jnp.zeros((tm, LANES), F32)
        for e, r in zip(es, pats):
            alpha = e * inv
            weight = jnp.where(low, alpha[:, 2 * p:2 * p + 1], alpha[:, 2 * p + 1:2 * p + 2])
            merged = merged + weight * r[:, lanes]
        y = y + _dot(merged.astype(BF16), w_ref[DN_WIDTH + p * LANES:DN_WIDTH + (p + 1) * LANES, :])
    y = y + _dot(c_ref[...], w_ref[DN_WIDTH + SWA_WIDTH:, :])
    o_ref[...] = y


def _mix_out(h, out_a, swa_outs, out_c, w_out, tm=512):
    t, d = h.shape

    def row(w):
        return pl.BlockSpec((tm, w), lambda i: (i, 0))

    return pl.pallas_call(
        functools.partial(_mix_out_kernel, tm=tm),
        grid=(t // tm,),
        in_specs=[row(d), row(DN_WIDTH), row(SWA_OUT), row(SWA_OUT), row(SWA_OUT), row(SG_WIDTH),
                  _resident(w_out.shape)],
        out_specs=row(d),
        out_shape=jax.ShapeDtypeStruct((t, d), F32),
        compiler_params=_params("parallel"),
        name="mix_out",
    )(h, out_a, *swa_outs, out_c, w_out)


def _kv_kernel(mem_ref, g_ref, w_ref, k_ref, v_ref):
    n = _rms(mem_ref[...], g_ref[...]).astype(BF16)
    d = k_ref.shape[-1]
    k_ref[...] = _dot(n, w_ref[:, :d]).astype(BF16)
    v_ref[...] = _dot(n, w_ref[:, d:]).astype(BF16)


def _mem_kv(mem, gain, w_kv):
    b, m, d = mem.shape
    blk = pl.BlockSpec((None, m, d), lambda i: (i, 0, 0))
    return pl.pallas_call(
        _kv_kernel,
        grid=(b,),
        in_specs=[blk, pl.BlockSpec((1, d), lambda i: (0, 0)), _resident((d, 2 * d))],
        out_specs=[blk, blk],
        out_shape=[jax.ShapeDtypeStruct((b, m, d), BF16)] * 2,
        compiler_params=_params("parallel"),
        name="xa_kv",
    )(mem, gain.reshape(1, d), w_kv)


def _xa_kernel(h_ref, g_ref, wq_ref, k_ref, v_ref, wo_ref, o_ref, att_scr):
    x = h_ref[...]
    n = _rms(x, g_ref[...]).astype(BF16)
    d = x.shape[-1]
    dh = d // XA_HEADS
    q = _dot(n, wq_ref[...]).astype(BF16)
    for h in range(XA_HEADS):
        lanes = slice(h * dh, (h + 1) * dh)
        s = _dot_nt(q[:, lanes], k_ref[:, lanes]) * (dh ** -0.5)
        m = jnp.max(s, axis=-1, keepdims=True)
        p = jnp.exp(s - m)
        p = p * (1.0 / jnp.sum(p, axis=-1, keepdims=True))
        att_scr[:, lanes] = _dot(p.astype(BF16), v_ref[:, lanes]).astype(BF16)
    o_ref[...] = x + _dot(att_scr[...], wo_ref[...])


def _cross_attention(h, gain, w_q, k, v, w_o, batch, tm=512):
    t, d = h.shape
    ns = t // batch // tm
    m = k.shape[1]
    row = pl.BlockSpec((tm, d), lambda b, s: (b * ns + s, 0))
    kv = pl.BlockSpec((None, m, d), lambda b, s: (b, 0, 0))
    return pl.pallas_call(
        _xa_kernel,
        grid=(batch, ns),
        in_specs=[row, pl.BlockSpec((1, d), lambda b, s: (0, 0)), _resident((d, d)), kv, kv, _resident((d, d))],
        out_specs=row,
        out_shape=jax.ShapeDtypeStruct((t, d), F32),
        scratch_shapes=[pltpu.VMEM((tm, d), BF16)],
        compiler_params=_params("parallel", "parallel"),
        name="cross_attn",
    )(h, gain.reshape(1, d), w_q, k, v, w_o)


def _mix_in_weights(w_in):
    o_z = 3 * DN_WIDTH
    o_b = o_z + DN_WIDTH
    o_swa = o_b + 2 * DN_HEADS
    o_sg = o_swa + 3 * SWA_WIDTH
    pad = jnp.zeros((w_in.shape[0], LANES - 2 * DN_HEADS), w_in.dtype)
    return jnp.concatenate([w_in[:, :o_b], w_in[:, o_swa:], w_in[:, o_b:o_swa], pad], axis=1).astype(BF16)


def kernel(x, mem, ffn1_norm, ffn1_w_gate_up, ffn1_w_down, mix_norm, mix_w_in, dn_conv_w, dn_a_log, dn_dt_bias,
           dn_out_norm, sg_norm_gain, sg_norm_bias, sg_w_spatial, sg_b_spatial, mix_w_out, xa_norm, xa_mem_norm,
           xa_w_q, xa_w_kv, xa_w_o, ffn2_norm, ffn2_w_gate_up, ffn2_w_down, final_norm):
    batch, seq, d = x.shape
    depth = ffn1_norm.shape[0]
    h = x.reshape(batch * seq, d)
    for i in range(depth):
        h = _ffn(h, ffn1_norm[i], ffn1_w_gate_up[i].astype(BF16), ffn1_w_down[i].astype(BF16))

        gate_params = jnp.zeros((8, LANES), F32)
        gate_params = gate_params.at[0, DN_HEADS:2 * DN_HEADS].set(dn_a_log[i])
        gate_params = gate_params.at[1, DN_HEADS:2 * DN_HEADS].set(dn_dt_bias[i])
        b_sp_rows = jnp.repeat(sg_b_spatial[i].T, SG_DIM, axis=1)
        qkv, z, ba, swa_qkv, out_c = _proj(h, mix_norm[i], _mix_in_weights(mix_w_in[i]), sg_norm_gain[i],
                                           sg_norm_bias[i], sg_w_spatial[i], b_sp_rows)
        out_a = _deltanet(qkv, z, ba, dn_conv_w[i], gate_params, dn_out_norm[i], batch)
        swa_outs = [_swa_pattern(swa_qkv, batch, seq, dil) for _, dil in SWA_PATTERNS]
        h = _mix_out(h, out_a, swa_outs, out_c, mix_w_out[i].astype(BF16))

        k, v = _mem_kv(mem, xa_mem_norm[i], xa_w_kv[i].astype(BF16))
        h = _cross_attention(h, xa_norm[i], xa_w_q[i].astype(BF16), k, v, xa_w_o[i].astype(BF16), batch)

        last = i == depth - 1
        h = _ffn(h, ffn2_norm[i], ffn2_w_gate_up[i].astype(BF16), ffn2_w_down[i].astype(BF16),
                 final_gain=final_norm if last else None)
    return h.reshape(batch, seq, d)
```

```python
import functools

import jax
import jax.numpy as jnp
from jax import lax
from jax.experimental import pallas as pl
from jax.experimental.pallas import tpu as pltpu

F32 = jnp.float32
BF16 = jnp.bfloat16
NORM_EPS = 1e-6
LANES = 128
VMEM_LIMIT = 56 * 1024 * 1024

DN_HEADS = 4
DN_DIM = 128
DN_WIDTH = DN_HEADS * DN_DIM
DN_CONV = 4
DN_CHUNK = 64
SWA_HEADS = 4
SWA_DIM = 64
SWA_WIDTH = SWA_HEADS * SWA_DIM
SWA_PATTERNS = ((128, 1), (512, 4), (2048, 16))
SWA_BLOCK = 128
SG_GROUPS = 4
SG_DIM = 64
SG_WIDTH = SG_GROUPS * SG_DIM
SG_CHUNK = 128
XA_HEADS = 4
NEG_BIG = -1e30


def _resident(shape):
    return pl.BlockSpec(shape, lambda *_: (0,) * len(shape), pipeline_mode=pl.Buffered(1))


def _params(*sem):
    return pltpu.CompilerParams(dimension_semantics=sem, vmem_limit_bytes=VMEM_LIMIT)


def _rms(x, gain):
    ms = jnp.mean(x * x, axis=-1, keepdims=True)
    return x * lax.rsqrt(ms + NORM_EPS) * gain


def _dot(a, b):
    return jnp.dot(a, b, preferred_element_type=F32)


def _dot_nt(a, b):
    return lax.dot_general(a, b, (((1,), (1,)), ((), ())), preferred_element_type=F32)


def _dot_tn(a, b):
    return lax.dot_general(a, b, (((0,), (0,)), ((), ())), preferred_element_type=F32)


def _silu(x):
    return x * jax.nn.sigmoid(x)


def _ffn_kernel(*refs, d_ff, chunk, final):
    if final:
        x_ref, g_ref, wgu_ref, wd_ref, fg_ref, o_ref, h_scr = refs
    else:
        x_ref, g_ref, wgu_ref, wd_ref, o_ref, h_scr = refs
    x = x_ref[...]
    n = _rms(x, g_ref[...]).astype(BF16)
    for c in range(d_ff // chunk):
        gate = _dot(n, wgu_ref[:, c * chunk:(c + 1) * chunk])
        up = _dot(n, wgu_ref[:, d_ff + c * chunk:d_ff + (c + 1) * chunk])
        h_scr[:, c * chunk:(c + 1) * chunk] = (_silu(gate) * up).astype(BF16)
    y = x + 0.5 * _dot(h_scr[...], wd_ref[...])
    if final:
        y = _rms(y, fg_ref[...])
    o_ref[...] = y


def _ffn(h, gain, w_gu, w_d, final_gain=None, tm=512, chunk=256):
    t, d = h.shape
    d_ff = w_d.shape[0]
    final = final_gain is not None
    row = pl.BlockSpec((tm, d), lambda i: (i, 0))
    vec = pl.BlockSpec((1, d), lambda i: (0, 0))
    in_specs = [row, vec, _resident((d, 2 * d_ff)), _resident((d_ff, d))]
    args = [h, gain.reshape(1, d), w_gu, w_d]
    if final:
        in_specs.append(vec)
        args.append(final_gain.reshape(1, d))
    return pl.pallas_call(
        functools.partial(_ffn_kernel, d_ff=d_ff, chunk=chunk, final=final),
        grid=(t // tm,),
        in_specs=in_specs,
        out_specs=row,
        out_shape=jax.ShapeDtypeStruct((t, d), F32),
        scratch_shapes=[pltpu.VMEM((tm, d_ff), BF16)],
        compiler_params=_params("parallel"),
        name="ffn_final" if final else "ffn",
    )(*args)


def _proj_kernel(x_ref, g_ref, w_ref, lg_ref, lb_ref, wsp_ref, bsp_ref,
                 qkv_ref, z_ref, ba_ref, swa_ref, c_ref, *, tm):
    n = _rms(x_ref[...], g_ref[...]).astype(BF16)
    o_z = 3 * DN_WIDTH
    o_swa = o_z + DN_WIDTH
    o_sg = o_swa + 3 * SWA_WIDTH
    o_ba = o_sg + 2 * SG_WIDTH
    qkv_ref[...] = _dot(n, w_ref[:, 0:o_z])
    z_ref[...] = _dot(n, w_ref[:, o_z:o_swa])
    swa_ref[...] = _dot(n, w_ref[:, o_swa:o_sg])
    ba_ref[...] = _dot(n, w_ref[:, o_ba:o_ba + LANES])

    uv = jax.nn.gelu(_dot(n, w_ref[:, o_sg:o_ba]))
    u = uv[:, :SG_WIDTH]
    v = uv[:, SG_WIDTH:]
    mu = jnp.mean(v, axis=-1, keepdims=True)
    vc = v - mu
    var = jnp.mean(vc * vc, axis=-1, keepdims=True)
    vn = vc * lax.rsqrt(var + NORM_EPS) * lg_ref[...] + lb_ref[...]
    row = lax.broadcasted_iota(jnp.int32, (SG_CHUNK, SG_CHUNK), 0)
    col = lax.broadcasted_iota(jnp.int32, (SG_CHUNK, SG_CHUNK), 1)
    causal = row >= col
    low_lanes = col < SG_DIM
    w_causal = [jnp.where(causal, wsp_ref[g], 0.0).astype(BF16) for g in range(SG_GROUPS)]
    for c in range(tm // SG_CHUNK):
        rows = slice(c * SG_CHUNK, (c + 1) * SG_CHUNK)
        for p in range(SG_GROUPS // 2):
            lanes = slice(p * LANES, (p + 1) * LANES)
            vp = vn[rows, lanes]
            v_lo = jnp.where(low_lanes, vp, 0.0).astype(BF16)
            v_hi = jnp.where(low_lanes, 0.0, vp).astype(BF16)
            mixed = _dot(w_causal[2 * p], v_lo) + _dot(w_causal[2 * p + 1], v_hi) + bsp_ref[:, lanes]
            c_ref[rows, lanes] = (u[rows, lanes] * mixed).astype(BF16)


def _proj(h, gain, w_all, sg_gain, sg_bias, w_sp, b_sp_rows, tm=512):
    t, d = h.shape
    ncols = w_all.shape[1]

    def row(w):
        return pl.BlockSpec((tm, w), lambda i: (i, 0))

    return pl.pallas_call(
        functools.partial(_proj_kernel, tm=tm),
        grid=(t // tm,),
        in_specs=[row(d), pl.BlockSpec((1, d), lambda i: (0, 0)), _resident((d, ncols)),
                  pl.BlockSpec((1, SG_WIDTH), lambda i: (0, 0)),
                  pl.BlockSpec((1, SG_WIDTH), lambda i: (0, 0)),
                  _resident((SG_GROUPS, SG_CHUNK, SG_CHUNK)),
                  _resident((SG_CHUNK, SG_WIDTH))],
        out_specs=[row(3 * DN_WIDTH), row(DN_WIDTH), row(LANES), row(3 * SWA_WIDTH), row(SG_WIDTH)],
        out_shape=[jax.ShapeDtypeStruct((t, 3 * DN_WIDTH), F32),
                   jax.ShapeDtypeStruct((t, DN_WIDTH), F32),
                   jax.ShapeDtypeStruct((t, LANES), F32),
                   jax.ShapeDtypeStruct((t, 3 * SWA_WIDTH), F32),
                   jax.ShapeDtypeStruct((t, SG_WIDTH), BF16)],
        compiler_params=_params("parallel"),
        name="mix_proj",
    )(h, gain.reshape(1, d), w_all, sg_gain.reshape(1, -1), sg_bias.reshape(1, -1), w_sp, b_sp_rows)


def _split3_bf16(a):
    hi = a.astype(BF16)
    r1 = a - hi.astype(F32)
    mid = r1.astype(BF16)
    lo = (r1 - mid.astype(F32)).astype(BF16)
    return hi, mid, lo


def _dot_exact_lhs(lhs16, rhs):
    hi, mid, lo = _split3_bf16(rhs)
    return _dot(lhs16, hi) + _dot(lhs16, mid) + _dot(lhs16, lo)


def _dn_kernel(qkv_ref, z_ref, ba_ref, cw_ref, gp_ref, og_ref, o_ref, xs_ref, st_ref, *, ts, unroll):
    c_len = DN_CHUNK
    halo = 8
    width = 3 * DN_WIDTH
    heads = range(DN_HEADS)

    @pl.when(pl.program_id(1) == 0)
    def _():
        xs_ref[0:halo, :] = jnp.zeros((halo, width), F32)
        st_ref[...] = jnp.zeros_like(st_ref)

    xs_ref[halo:halo + ts, :] = qkv_ref[...]

    neg_a = -jnp.exp(gp_ref[0:1, :])
    dt_bias = gp_ref[1:2, :]
    out_gain = og_ref[...]
    cw = cw_ref[...]

    row = lax.broadcasted_iota(jnp.int32, (c_len, c_len), 0)
    col = lax.broadcasted_iota(jnp.int32, (c_len, c_len), 1)
    tri_incl = row >= col
    tri_strict = row > col
    eye = (row == col).astype(F32)
    tri16 = tri_incl.astype(BF16)
    ones16 = jnp.ones((c_len, c_len), BF16)
    base_mask = ((row >> 1) == (col >> 1)).astype(F32)
    level_masks = []
    sh = 1
    while (1 << sh) < c_len:
        level_masks.append((((row >> (sh + 1)) == (col >> (sh + 1))) & ((row >> sh) != (col >> sh))).astype(F32))
        sh += 1
    rowh = lax.broadcasted_iota(jnp.int32, (c_len, DN_HEADS * LANES), 0)
    colh = lax.broadcasted_iota(jnp.int32, (c_len, DN_HEADS * LANES), 1) & (LANES - 1)
    upto = (rowh <= colh).astype(F32)

    def prepare(r0):
        x = xs_ref[pl.ds(r0, c_len + halo), :]
        acc = cw[DN_CONV - 1:DN_CONV, :] * x[halo:halo + c_len]
        for j in range(DN_CONV - 1):
            lo = halo - (DN_CONV - 1) + j
            acc = acc + cw[j:j + 1, :] * x[lo:lo + c_len]
        a = _silu(acc)

        ba = ba_ref[pl.ds(r0, c_len), :]
        beta_all = jax.nn.sigmoid(ba)
        g_all = neg_a * jax.nn.softplus(ba + dt_bias)
        gseg = jnp.concatenate(
            [jnp.broadcast_to(g_all[:, DN_HEADS + h:DN_HEADS + h + 1], (c_len, LANES)) for h in heads], axis=1)
        gcol = _dot_exact_lhs(tri16, gseg)
        grow = _dot_exact_lhs(ones16, gseg * upto)

        def head_lanes(base, h):
            return a[:, base + h * DN_DIM:base + (h + 1) * DN_DIM]

        q = [head_lanes(0, h) for h in heads]
        k = [head_lanes(DN_WIDTH, h) for h in heads]
        v = [head_lanes(2 * DN_WIDTH, h) for h in heads]
        q = [t * (lax.rsqrt(jnp.sum(t * t, axis=-1, keepdims=True) + NORM_EPS) * (DN_DIM ** -0.5)) for t in q]
        k = [t * lax.rsqrt(jnp.sum(t * t, axis=-1, keepdims=True) + NORM_EPS) for t in k]
        beta = [beta_all[:, h:h + 1] for h in heads]
        gcb = [gcol[:, h * LANES:(h + 1) * LANES] for h in heads]
        grow_h = [grow[:, h * LANES:(h + 1) * LANES] for h in heads]
        g_last = [t[:, c_len:c_len + 1] for t in grow_h]
        decay = [jnp.where(tri_incl, jnp.exp(jnp.where(tri_incl, gc[:, :c_len] - gr[:, :c_len], 0.0)), 0.0)
                 for gc, gr in zip(gcb, grow_h)]
        kb = [t * b for t, b in zip(k, beta)]
        k16 = [t.astype(BF16) for t in k]
        lower = [jnp.where(tri_strict, _dot_nt(t.astype(BF16), t16) * d, 0.0) for t, t16, d in zip(kb, k16, decay)]
        a_qk = [jnp.where(tri_incl, _dot_nt(t.astype(BF16), t16) * d, 0.0).astype(BF16)
                for t, t16, d in zip(q, k16, decay)]
        t_inv = [eye - lo * base_mask for lo in lower]
        for m in level_masks:
            t16 = [t.astype(BF16) for t in t_inv]
            left = [_dot(t, (lo * m).astype(BF16)).astype(BF16) for t, lo in zip(t16, lower)]
            t_inv = [t - _dot(l, t_b) for t, l, t_b in zip(t_inv, left, t16)]
        t16 = [t.astype(BF16) for t in t_inv]
        egc = [jnp.exp(t) for t in gcb]
        u = [_dot(t, (vv * b).astype(BF16)) for t, vv, b in zip(t16, v, beta)]
        w = [_dot(t, (kk * e).astype(BF16)).astype(BF16) for t, kk, e in zip(t16, kb, egc)]
        k_tail = [(kk * jnp.exp(gl - gc)).astype(BF16) for kk, gl, gc in zip(k, g_last, gcb)]
        q_dec = [(qq * e).astype(BF16) for qq, e in zip(q, egc)]
        s_decay = [jnp.exp(gl[0:1, :]) for gl in g_last]
        return u, w, a_qk, k_tail, q_dec, s_decay

    def trip(c, carry):
        starts = [pl.multiple_of((c * unroll + j) * c_len, c_len) for j in range(unroll)]
        prepared = [prepare(r0) for r0 in starts]
        state = [st_ref[h] for h in heads]
        for r0, (u, w, a_qk, k_tail, q_dec, s_decay) in zip(starts, prepared):
            s16 = [s.astype(BF16) for s in state]
            v_new = [uu - _dot(ww, s) for uu, ww, s in zip(u, w, s16)]
            vn16 = [t.astype(BF16) for t in v_new]
            state = [s * d + _dot_tn(kt, vn) for s, d, kt, vn in zip(state, s_decay, k_tail, vn16)]
            o = [_dot(qd, s) + _dot(aq, vn) for qd, s, aq, vn in zip(q_dec, s16, a_qk, vn16)]
            o = [t * lax.rsqrt(jnp.mean(t * t, axis=-1, keepdims=True) + NORM_EPS) * out_gain for t in o]
            for h in heads:
                lanes = slice(h * DN_DIM, (h + 1) * DN_DIM)
                zh = z_ref[pl.ds(r0, c_len), lanes]
                o_ref[pl.ds(r0, c_len), lanes] = (o[h] * _silu(zh)).astype(o_ref.dtype)
        for h in heads:
            st_ref[h] = state[h]
        return carry

    lax.fori_loop(0, ts // (c_len * unroll), trip, 0)
    xs_ref[0:halo, :] = xs_ref[ts:ts + halo, :]


def _deltanet(qkv, z, ba, conv_w, gate_params, out_gain, batch, ts=512, unroll=2):
    t = qkv.shape[0]
    ns = t // batch // ts

    def row(w):
        return pl.BlockSpec((ts, w), lambda b, s: (b * ns + s, 0))

    return pl.pallas_call(
        functools.partial(_dn_kernel, ts=ts, unroll=unroll),
        grid=(batch, ns),
        in_specs=[row(3 * DN_WIDTH), row(DN_WIDTH), row(LANES),
                  pl.BlockSpec((DN_CONV, 3 * DN_WIDTH), lambda b, s: (0, 0)),
                  pl.BlockSpec((8, LANES), lambda b, s: (0, 0)),
                  pl.BlockSpec((1, DN_DIM), lambda b, s: (0, 0))],
        out_specs=row(DN_WIDTH),
        out_shape=jax.ShapeDtypeStruct((t, DN_WIDTH), BF16),
        scratch_shapes=[pltpu.VMEM((ts + 8, 3 * DN_WIDTH), F32),
                        pltpu.VMEM((DN_HEADS, DN_DIM, DN_DIM), F32)],
        compiler_params=_params("parallel", "arbitrary"),
        name="deltanet",
    )(qkv, z, ba, conv_w, gate_params, out_gain.reshape(1, DN_DIM))


def _swa_bias(slopes, dilation):
    blk = SWA_BLOCK
    shape = (2 * blk, 2 * blk)
    row = lax.broadcasted_iota(jnp.int32, shape, 0)
    col = lax.broadcasted_iota(jnp.int32, shape, 1)
    rel = (row & (blk - 1)) + blk - col
    slope = jnp.where(row < blk, slopes[0] * dilation, slopes[1] * dilation)
    return jnp.where((rel >= 0) & (rel <= blk), -slope * rel.astype(F32), NEG_BIG)


def _swa_kernel(q0_ref, q1_ref, k0_ref, k1_ref, v0_ref, v1_ref, o_ref, kv_scr, acc_scr, lse_scr, *, tq, group):
    blk = SWA_BLOCK
    q_refs = (q0_ref, q1_ref)
    pairs = range(SWA_HEADS // 2)
    n_kv = 2 * len(pairs)
    have_prev = pl.program_id(1) > 0

    @pl.when(jnp.logical_not(have_prev))
    def _():
        kv_scr[:, 0:tq, :] = jnp.zeros((n_kv, tq, LANES), F32)

    for c, ref in enumerate((k0_ref, k1_ref, v0_ref, v1_ref)):
        kv_scr[c, tq:2 * tq, :] = ref[...]

    low = lax.broadcasted_iota(jnp.int32, (blk, LANES), 1) < SWA_DIM
    same_block = lax.broadcasted_iota(jnp.int32, (2 * blk, 2 * blk), 1) >= blk
    slopes = [2.0 ** (-8.0 * (h + 1) / SWA_HEADS) for h in range(SWA_HEADS)]

    def run_group(pi, dilation, bias, starts, prev_valid):
        units = range(len(starts))
        if dilation > 1:
            q_rows = [pl.ds(s, blk, stride=dilation) for s in starts]
            kv_rows = [pl.ds(tq + s - dilation * blk, 2 * blk, stride=dilation) for s in starts]
        else:
            q_rows = [pl.ds(s, blk) for s in starts]
            kv_rows = [pl.ds(tq + s - blk, 2 * blk) for s in starts]
        up = [(u, p) for u in units for p in pairs]
        q = [(q_refs[p][q_rows[u], :] * (SWA_DIM ** -0.5)).astype(BF16) for u, p in up]
        zero = jnp.zeros_like(q[0])
        q2 = [jnp.concatenate([jnp.where(low, t, zero), jnp.where(low, zero, t)], axis=0) for t in q]
        k2 = [kv_scr[p, kv_rows[u], :].astype(BF16) for u, p in up]
        v2 = [kv_scr[len(pairs) + p, kv_rows[u], :].astype(BF16) for u, p in up]

        def unit_bias(u, p):
            if prev_valid[u] is None:
                return bias[p]
            return jnp.where(jnp.logical_or(same_block, prev_valid[u]), bias[p], NEG_BIG)

        s = [_dot_nt(a, b) + unit_bias(u, p) for (u, p), a, b in zip(up, q2, k2)]
        m = [jnp.max(t, axis=-1, keepdims=True) for t in s]
        e = [jnp.exp(t - mm) for t, mm in zip(s, m)]
        den = [jnp.sum(t, axis=-1, keepdims=True) for t in e]
        pv = [_dot(t.astype(BF16), b) for t, b in zip(e, v2)]
        for (u, p), mm, dd, o in zip(up, m, den, pv):
            inv = 1.0 / dd
            lse = mm + jnp.log(dd)
            acc_scr[pi, p, q_rows[u], :] = jnp.where(low, o[:blk] * inv[:blk], o[blk:] * inv[blk:])
            lse_scr[pi, p, q_rows[u], :] = jnp.where(low, lse[:blk], lse[blk:])

    for pi, (_, dilation) in enumerate(SWA_PATTERNS):
        bias = [_swa_bias(slopes[2 * p:2 * p + 2], dilation) for p in pairs]
        per_residue = tq // (dilation * blk)
        if per_residue >= group:
            n_trips = per_residue // group
            for r in range(dilation):
                def trip(g, carry, r=r, pi=pi, dilation=dilation, bias=bias):
                    starts = [r + dilation * blk * (g * group + u) for u in range(group)]
                    first = jnp.logical_or(have_prev, g > 0)
                    run_group(pi, dilation, bias, starts, [first] + [None] * (group - 1))
                    return carry
                if n_trips == 1:
                    trip(0, 0)
                else:
                    lax.fori_loop(0, n_trips, trip, 0)
        else:
            assert per_residue == 1 and dilation % group == 0
            for r0 in range(0, dilation, group):
                run_group(pi, dilation, bias, [r0 + u for u in range(group)], [have_prev] * group)

    step = 256
    n_pat = len(SWA_PATTERNS)
    for c in range(tq // step):
        rows = slice(c * step, (c + 1) * step)
        for p in pairs:
            lses = [lse_scr[pi, p, rows, :] for pi in range(n_pat)]
            top = functools.reduce(jnp.maximum, lses)
            es = [jnp.exp(l - top) for l in lses]
            inv = 1.0 / functools.reduce(lambda a, b: a + b, es)
            merged = functools.reduce(lambda a, b: a + b, [es[pi] * acc_scr[pi, p, rows, :] for pi in range(n_pat)])
            o_ref[rows, p * LANES:(p + 1) * LANES] = (merged * inv).astype(o_ref.dtype)

    kv_scr[:, 0:tq, :] = kv_scr[:, tq:2 * tq, :]


def _swa(swa_qkv, batch, tq=2048, group=4):
    t = swa_qkv.shape[0]
    nj = t // batch // tq
    n_tiles = SWA_WIDTH // LANES
    n_pat = len(SWA_PATTERNS)
    return pl.pallas_call(
        functools.partial(_swa_kernel, tq=tq, group=group),
        grid=(batch, nj),
        in_specs=[pl.BlockSpec((tq, LANES), functools.partial(lambda b, j, c: (b * nj + j, c), c=c))
                  for c in range(3 * n_tiles)],
        out_specs=pl.BlockSpec((tq, SWA_WIDTH), lambda b, j: (b * nj + j, 0)),
        out_shape=jax.ShapeDtypeStruct((t, SWA_WIDTH), BF16),
        scratch_shapes=[pltpu.VMEM((2 * n_tiles, 2 * tq, LANES), F32),
                        pltpu.VMEM((n_pat, n_tiles, tq, LANES), F32),
                        pltpu.VMEM((n_pat, n_tiles, tq, LANES), F32)],
        compiler_params=_params("parallel", "arbitrary"),
        name="swa",
    )(*([swa_qkv] * (3 * n_tiles)))


def _mix_out_kernel(h_ref, a_ref, b_ref, c_ref, w_ref, o_ref):
    y = h_ref[...] + _dot(a_ref[...], w_ref[0:DN_WIDTH, :])
    y = y + _dot(b_ref[...], w_ref[DN_WIDTH:DN_WIDTH + SWA_WIDTH, :])
    y = y + _dot(c_ref[...], w_ref[DN_WIDTH + SWA_WIDTH:, :])
    o_ref[...] = y


def _mix_out(h, out_a, out_b, out_c, w_out, tm=512):
    t, d = h.shape

    def row(w):
        return pl.BlockSpec((tm, w), lambda i: (i, 0))

    return pl.pallas_call(
        _mix_out_kernel,
        grid=(t // tm,),
        in_specs=[row(d), row(DN_WIDTH), row(SWA_WIDTH), row(SG_WIDTH), _resident(w_out.shape)],
        out_specs=row(d),
        out_shape=jax.ShapeDtypeStruct((t, d), F32),
        compiler_params=_params("parallel"),
        name="mix_out",
    )(h, out_a, out_b, out_c, w_out)


def _kv_kernel(mem_ref, g_ref, w_ref, k_ref, v_ref):
    n = _rms(mem_ref[...], g_ref[...]).astype(BF16)
    d = k_ref.shape[-1]
    k_ref[...] = _dot(n, w_ref[:, :d]).astype(BF16)
    v_ref[...] = _dot(n, w_ref[:, d:]).astype(BF16)


def _mem_kv(mem, gain, w_kv):
    b, m, d = mem.shape
    blk = pl.BlockSpec((None, m, d), lambda i: (i, 0, 0))
    return pl.pallas_call(
        _kv_kernel,
        grid=(b,),
        in_specs=[blk, pl.BlockSpec((1, d), lambda i: (0, 0)), _resident((d, 2 * d))],
        out_specs=[blk, blk],
        out_shape=[jax.ShapeDtypeStruct((b, m, d), BF16)] * 2,
        compiler_params=_params("parallel"),
        name="xa_kv",
    )(mem, gain.reshape(1, d), w_kv)


def _xa_kernel(h_ref, g_ref, wq_ref, k_ref, v_ref, wo_ref, o_ref, att_scr):
    x = h_ref[...]
    n = _rms(x, g_ref[...]).astype(BF16)
    d = x.shape[-1]
    dh = d // XA_HEADS
    q = _dot(n, wq_ref[...]).astype(BF16)
    for h in range(XA_HEADS):
        lanes = slice(h * dh, (h + 1) * dh)
        s = _dot_nt(q[:, lanes], k_ref[:, lanes]) * (dh ** -0.5)
        m = jnp.max(s, axis=-1, keepdims=True)
        p = jnp.exp(s - m)
        p = p * (1.0 / jnp.sum(p, axis=-1, keepdims=True))
        att_scr[:, lanes] = _dot(p.astype(BF16), v_ref[:, lanes]).astype(BF16)
    o_ref[...] = x + _dot(att_scr[...], wo_ref[...])


def _cross_attention(h, gain, w_q, k, v, w_o, batch, tm=512):
    t, d = h.shape
    ns = t // batch // tm
    m = k.shape[1]
    row = pl.BlockSpec((tm, d), lambda b, s: (b * ns + s, 0))
    kv = pl.BlockSpec((None, m, d), lambda b, s: (b, 0, 0))
    return pl.pallas_call(
        _xa_kernel,
        grid=(batch, ns),
        in_specs=[row, pl.BlockSpec((1, d), lambda b, s: (0, 0)), _resident((d, d)), kv, kv, _resident((d, d))],
        out_specs=row,
        out_shape=jax.ShapeDtypeStruct((t, d), F32),
        scratch_shapes=[pltpu.VMEM((tm, d), BF16)],
        compiler_params=_params("parallel", "parallel"),
        name="cross_attn",
    )(h, gain.reshape(1, d), w_q, k, v, w_o)


def _mix_in_weights(w_in):
    o_z = 3 * DN_WIDTH
    o_b = o_z + DN_WIDTH
    o_swa = o_b + 2 * DN_HEADS
    o_sg = o_swa + 3 * SWA_WIDTH
    pad = jnp.zeros((w_in.shape[0], LANES - 2 * DN_HEADS), w_in.dtype)
    return jnp.concatenate([w_in[:, :o_b], w_in[:, o_swa:], w_in[:, o_b:o_swa], pad], axis=1).astype(BF16)


def kernel(x, mem, ffn1_norm, ffn1_w_gate_up, ffn1_w_down, mix_norm, mix_w_in, dn_conv_w, dn_a_log, dn_dt_bias,
           dn_out_norm, sg_norm_gain, sg_norm_bias, sg_w_spatial, sg_b_spatial, mix_w_out, xa_norm, xa_mem_norm,
           xa_w_q, xa_w_kv, xa_w_o, ffn2_norm, ffn2_w_gate_up, ffn2_w_down, final_norm):
    batch, seq, d = x.shape
    depth = ffn1_norm.shape[0]
    h = x.reshape(batch * seq, d)
    for i in range(depth):
        h = _ffn(h, ffn1_norm[i], ffn1_w_gate_up[i].astype(BF16), ffn1_w_down[i].astype(BF16))

        gate_params = jnp.zeros((8, LANES), F32)
        gate_params = gate_params.at[0, DN_HEADS:2 * DN_HEADS].set(dn_a_log[i])
        gate_params = gate_params.at[1, DN_HEADS:2 * DN_HEADS].set(dn_dt_bias[i])
        b_sp_rows = jnp.repeat(sg_b_spatial[i].T, SG_DIM, axis=1)
        qkv, z, ba, swa_qkv, out_c = _proj(h, mix_norm[i], _mix_in_weights(mix_w_in[i]), sg_norm_gain[i],
                                           sg_norm_bias[i], sg_w_spatial[i], b_sp_rows)
        out_a = _deltanet(qkv, z, ba, dn_conv_w[i], gate_params, dn_out_norm[i], batch)
        out_b = _swa(swa_qkv, batch)
        h = _mix_out(h, out_a, out_b, out_c, mix_w_out[i].astype(BF16))

        k, v = _mem_kv(mem, xa_mem_norm[i], xa_w_kv[i].astype(BF16))
        h = _cross_attention(h, xa_norm[i], xa_w_q[i].astype(BF16), k, v, xa_w_o[i].astype(BF16), batch)

        last = i == depth - 1
        h = _ffn(h, ffn2_norm[i], ffn2_w_gate_up[i].astype(BF16), ffn2_w_down[i].astype(BF16),
                 final_gain=final_norm if last else None)
    return h.reshape(batch, seq, d)
```

```python
import functools

import jax
import jax.numpy as jnp
from jax import lax
from jax.experimental import pallas as pl
from jax.experimental.pallas import tpu as pltpu

F32 = jnp.float32
BF16 = jnp.bfloat16
NORM_EPS = 1e-6
LANES = 128
VMEM_LIMIT = 56 * 1024 * 1024

DN_HEADS = 4
DN_DIM = 128
DN_WIDTH = DN_HEADS * DN_DIM
DN_CONV = 4
DN_CHUNK = 64
SWA_HEADS = 4
SWA_DIM = 64
SWA_WIDTH = SWA_HEADS * SWA_DIM
SWA_PATTERNS = ((128, 1), (512, 4), (2048, 16))
SWA_BLOCK = 128
SG_GROUPS = 4
SG_DIM = 64
SG_WIDTH = SG_GROUPS * SG_DIM
SG_CHUNK = 128
XA_HEADS = 4
NEG_BIG = -1e30


def _resident(shape):
    return pl.BlockSpec(shape, lambda *_: (0,) * len(shape), pipeline_mode=pl.Buffered(1))


def _params(*sem):
    return pltpu.CompilerParams(dimension_semantics=sem, vmem_limit_bytes=VMEM_LIMIT)


def _rms(x, gain):
    ms = jnp.mean(x * x, axis=-1, keepdims=True)
    return x * lax.rsqrt(ms + NORM_EPS) * gain


def _dot(a, b):
    return jnp.dot(a, b, preferred_element_type=F32)


def _dot_nt(a, b):
    return lax.dot_general(a, b, (((1,), (1,)), ((), ())), preferred_element_type=F32)


def _dot_tn(a, b):
    return lax.dot_general(a, b, (((0,), (0,)), ((), ())), preferred_element_type=F32)


def _silu(x):
    return x * jax.nn.sigmoid(x)


def _ffn_kernel(*refs, d_ff, chunk, final):
    if final:
        x_ref, g_ref, wgu_ref, wd_ref, fg_ref, o_ref, h_scr = refs
    else:
        x_ref, g_ref, wgu_ref, wd_ref, o_ref, h_scr = refs
    x = x_ref[...]
    n = _rms(x, g_ref[...]).astype(BF16)
    for c in range(d_ff // chunk):
        gate = _dot(n, wgu_ref[:, c * chunk:(c + 1) * chunk])
        up = _dot(n, wgu_ref[:, d_ff + c * chunk:d_ff + (c + 1) * chunk])
        h_scr[:, c * chunk:(c + 1) * chunk] = (_silu(gate) * up).astype(BF16)
    y = x + 0.5 * _dot(h_scr[...], wd_ref[...])
    if final:
        y = _rms(y, fg_ref[...])
    o_ref[...] = y


def _ffn(h, gain, w_gu, w_d, final_gain=None, tm=512, chunk=256):
    t, d = h.shape
    d_ff = w_d.shape[0]
    final = final_gain is not None
    row = pl.BlockSpec((tm, d), lambda i: (i, 0))
    vec = pl.BlockSpec((1, d), lambda i: (0, 0))
    in_specs = [row, vec, _resident((d, 2 * d_ff)), _resident((d_ff, d))]
    args = [h, gain.reshape(1, d), w_gu, w_d]
    if final:
        in_specs.append(vec)
        args.append(final_gain.reshape(1, d))
    return pl.pallas_call(
        functools.partial(_ffn_kernel, d_ff=d_ff, chunk=chunk, final=final),
        grid=(t // tm,),
        in_specs=in_specs,
        out_specs=row,
        out_shape=jax.ShapeDtypeStruct((t, d), F32),
        scratch_shapes=[pltpu.VMEM((tm, d_ff), BF16)],
        compiler_params=_params("parallel"),
        name="ffn_final" if final else "ffn",
    )(*args)


def _proj_kernel(x_ref, g_ref, w_ref, cw_ref, lg_ref, lb_ref, wsp_ref, bsp_ref,
                 qkv_ref, z_ref, ba_ref, swa_ref, c_ref, xs_ref, *, tm, tiles_per_seq):
    halo = 8
    piece = DN_CHUNK
    n = _rms(x_ref[...], g_ref[...]).astype(BF16)
    o_z = 3 * DN_WIDTH
    o_swa = o_z + DN_WIDTH
    o_sg = o_swa + 3 * SWA_WIDTH
    o_ba = o_sg + 2 * SG_WIDTH

    @pl.when(pl.program_id(0) % tiles_per_seq == 0)
    def _():
        xs_ref[0:halo, :] = jnp.zeros((halo, o_z), F32)

    cw = cw_ref[...]

    def stage(part):
        cols = slice(part * DN_WIDTH, (part + 1) * DN_WIDTH)
        xs_ref[halo:halo + tm, cols] = _dot(n, w_ref[:, cols])

    def dn_front(part):
        cols = slice(part * DN_WIDTH, (part + 1) * DN_WIDTH)
        for c in range(tm // piece):
            x = xs_ref[c * piece:c * piece + piece + halo, cols]
            acc = cw[DN_CONV - 1:DN_CONV, cols] * x[halo:halo + piece]
            for j in range(DN_CONV - 1):
                lo = halo - (DN_CONV - 1) + j
                acc = acc + cw[j:j + 1, cols] * x[lo:lo + piece]
            a = _silu(acc)
            rows = slice(c * piece, (c + 1) * piece)
            if part == 2:
                qkv_ref[rows, cols] = a
                continue
            for h in range(DN_HEADS):
                y = a[:, h * DN_DIM:(h + 1) * DN_DIM]
                inv = lax.rsqrt(jnp.sum(y * y, axis=-1, keepdims=True) + NORM_EPS)
                if part == 0:
                    inv = inv * (DN_DIM ** -0.5)
                qkv_ref[rows, part * DN_WIDTH + h * DN_DIM:part * DN_WIDTH + (h + 1) * DN_DIM] = y * inv

    stage(0)
    stage(1)
    dn_front(0)
    stage(2)
    dn_front(1)
    z_ref[...] = _silu(_dot(n, w_ref[:, o_z:o_swa]))
    swa_ref[...] = _dot(n, w_ref[:, o_swa:o_sg])
    dn_front(2)
    ba_ref[...] = _dot(n, w_ref[:, o_ba:o_ba + LANES])

    uv = jax.nn.gelu(_dot(n, w_ref[:, o_sg:o_ba]))
    u = uv[:, :SG_WIDTH]
    v = uv[:, SG_WIDTH:]
    mu = jnp.mean(v, axis=-1, keepdims=True)
    vc = v - mu
    var = jnp.mean(vc * vc, axis=-1, keepdims=True)
    vn = vc * lax.rsqrt(var + NORM_EPS) * lg_ref[...] + lb_ref[...]
    row = lax.broadcasted_iota(jnp.int32, (SG_CHUNK, SG_CHUNK), 0)
    col = lax.broadcasted_iota(jnp.int32, (SG_CHUNK, SG_CHUNK), 1)
    causal = row >= col
    low_lanes = col < SG_DIM
    w_causal = [jnp.where(causal, wsp_ref[g], 0.0).astype(BF16) for g in range(SG_GROUPS)]
    for c in range(tm // SG_CHUNK):
        rows = slice(c * SG_CHUNK, (c + 1) * SG_CHUNK)
        for p in range(SG_GROUPS // 2):
            lanes = slice(p * LANES, (p + 1) * LANES)
            vp = vn[rows, lanes]
            v_lo = jnp.where(low_lanes, vp, 0.0).astype(BF16)
            v_hi = jnp.where(low_lanes, 0.0, vp).astype(BF16)
            mixed = _dot(w_causal[2 * p], v_lo) + _dot(w_causal[2 * p + 1], v_hi) + bsp_ref[:, lanes]
            c_ref[rows, lanes] = (u[rows, lanes] * mixed).astype(BF16)

    xs_ref[0:halo, :] = xs_ref[tm:tm + halo, :]


def _proj(h, gain, w_all, conv_w, sg_gain, sg_bias, w_sp, b_sp_rows, seq, tm=512):
    t, d = h.shape
    ncols = w_all.shape[1]

    def row(w):
        return pl.BlockSpec((tm, w), lambda i: (i, 0))

    return pl.pallas_call(
        functools.partial(_proj_kernel, tm=tm, tiles_per_seq=seq // tm),
        grid=(t // tm,),
        in_specs=[row(d), pl.BlockSpec((1, d), lambda i: (0, 0)), _resident((d, ncols)),
                  pl.BlockSpec((DN_CONV, 3 * DN_WIDTH), lambda i: (0, 0)),
                  pl.BlockSpec((1, SG_WIDTH), lambda i: (0, 0)),
                  pl.BlockSpec((1, SG_WIDTH), lambda i: (0, 0)),
                  _resident((SG_GROUPS, SG_CHUNK, SG_CHUNK)),
                  _resident((SG_CHUNK, SG_WIDTH))],
        out_specs=[row(3 * DN_WIDTH), row(DN_WIDTH), row(LANES), row(3 * SWA_WIDTH), row(SG_WIDTH)],
        out_shape=[jax.ShapeDtypeStruct((t, 3 * DN_WIDTH), F32),
                   jax.ShapeDtypeStruct((t, DN_WIDTH), F32),
                   jax.ShapeDtypeStruct((t, LANES), F32),
                   jax.ShapeDtypeStruct((t, 3 * SWA_WIDTH), F32),
                   jax.ShapeDtypeStruct((t, SG_WIDTH), BF16)],
        scratch_shapes=[pltpu.VMEM((tm + 8, 3 * DN_WIDTH), F32)],
        compiler_params=_params("arbitrary"),
        name="mix_proj",
    )(h, gain.reshape(1, d), w_all, conv_w, sg_gain.reshape(1, -1), sg_bias.reshape(1, -1), w_sp, b_sp_rows)


def _split3_bf16(a):
    hi = a.astype(BF16)
    r1 = a - hi.astype(F32)
    mid = r1.astype(BF16)
    lo = (r1 - mid.astype(F32)).astype(BF16)
    return hi, mid, lo


def _dot_exact_lhs(lhs16, rhs):
    hi, mid, lo = _split3_bf16(rhs)
    return _dot(lhs16, hi) + _dot(lhs16, mid) + _dot(lhs16, lo)


def _dn_kernel(qkv_ref, z_ref, ba_ref, gp_ref, og_ref, o_ref, st_ref, *, ts, batch):
    c_len = DN_CHUNK
    heads = range(DN_HEADS)
    seqs = range(batch)

    @pl.when(pl.program_id(0) == 0)
    def _():
        st_ref[...] = jnp.zeros_like(st_ref)

    neg_a = -jnp.exp(gp_ref[0:1, :])
    dt_bias = gp_ref[1:2, :]
    out_gain = og_ref[...]

    row = lax.broadcasted_iota(jnp.int32, (c_len, c_len), 0)
    col = lax.broadcasted_iota(jnp.int32, (c_len, c_len), 1)
    tri_incl = row >= col
    tri_strict = row > col
    eye = (row == col).astype(F32)
    tri16 = tri_incl.astype(BF16)
    ones16 = jnp.ones((c_len, c_len), BF16)
    base_mask = ((row >> 1) == (col >> 1)).astype(F32)
    level_masks = []
    sh = 1
    while (1 << sh) < c_len:
        level_masks.append((((row >> (sh + 1)) == (col >> (sh + 1))) & ((row >> sh) != (col >> sh))).astype(F32))
        sh += 1
    rowh = lax.broadcasted_iota(jnp.int32, (c_len, DN_HEADS * LANES), 0)
    colh = lax.broadcasted_iota(jnp.int32, (c_len, DN_HEADS * LANES), 1) & (LANES - 1)
    upto = (rowh <= colh).astype(F32)

    def prepare(j):
        rows = slice(j * c_len, (j + 1) * c_len)
        units = [(b, h) for b in seqs for h in heads]
        beta_all, gcol, grow = [], [], []
        for b in seqs:
            ba = ba_ref[b, rows, :]
            beta_all.append(jax.nn.sigmoid(ba))
            g_all = neg_a * jax.nn.softplus(ba + dt_bias)
            gseg = jnp.concatenate(
                [jnp.broadcast_to(g_all[:, DN_HEADS + h:DN_HEADS + h + 1], (c_len, LANES)) for h in heads], axis=1)
            gcol.append(_dot_exact_lhs(tri16, gseg))
            grow.append(_dot_exact_lhs(ones16, gseg * upto))

        def head_lanes(b, base, h):
            return qkv_ref[b, rows, base + h * DN_DIM:base + (h + 1) * DN_DIM]

        q = [head_lanes(b, 0, h) for b, h in units]
        k = [head_lanes(b, DN_WIDTH, h) for b, h in units]
        v = [head_lanes(b, 2 * DN_WIDTH, h) for b, h in units]
        beta = [beta_all[b][:, h:h + 1] for b, h in units]
        gcb = [gcol[b][:, h * LANES:(h + 1) * LANES] for b, h in units]
        grow_h = [grow[b][:, h * LANES:(h + 1) * LANES] for b, h in units]
        g_last = [t[:, c_len:c_len + 1] for t in grow_h]
        decay = [jnp.where(tri_incl, jnp.exp(jnp.where(tri_incl, gc[:, :c_len] - gr[:, :c_len], 0.0)), 0.0)
                 for gc, gr in zip(gcb, grow_h)]
        kb = [t * bb for t, bb in zip(k, beta)]
        k16 = [t.astype(BF16) for t in k]
        both = [_dot_nt(jnp.concatenate([t, qq], axis=0).astype(BF16), t16) for t, qq, t16 in zip(kb, q, k16)]
        lower = [jnp.where(tri_strict, t[:c_len] * d, 0.0) for t, d in zip(both, decay)]
        a_qk = [jnp.where(tri_incl, t[c_len:] * d, 0.0).astype(BF16) for t, d in zip(both, decay)]
        t_inv = [eye - lo * base_mask for lo in lower]
        for m in level_masks:
            t16 = [t.astype(BF16) for t in t_inv]
            left = [_dot(t, (lo * m).astype(BF16)).astype(BF16) for t, lo in zip(t16, lower)]
            t_inv = [t - _dot(l, t_b) for t, l, t_b in zip(t_inv, left, t16)]
        t16 = [t.astype(BF16) for t in t_inv]
        egc = [jnp.exp(t) for t in gcb]
        uw = [_dot(t, jnp.concatenate([vv * bb, kk * e], axis=1).astype(BF16))
              for t, vv, bb, kk, e in zip(t16, v, beta, kb, egc)]
        u = [t[:, :DN_DIM] for t in uw]
        k_tail = [(kk * jnp.exp(gl - gc)).astype(BF16) for kk, gl, gc in zip(k, g_last, gcb)]
        wq = [jnp.concatenate([t[:, DN_DIM:], qq * e], axis=0).astype(BF16) for t, qq, e in zip(uw, q, egc)]
        s_decay = [jnp.exp(gl[0:1, :]) for gl in g_last]
        return u, wq, a_qk, k_tail, s_decay

    def advance(j, prepared, state):
        rows = slice(j * c_len, (j + 1) * c_len)
        units = [(b, h) for b in seqs for h in heads]
        u, wq, a_qk, k_tail, s_decay = prepared
        ws = [_dot(t, s.astype(BF16)) for t, s in zip(wq, state)]
        vn16 = [(uu - t[:c_len]).astype(BF16) for uu, t in zip(u, ws)]
        new_state = [s * d + _dot_tn(kt, vn) for s, d, kt, vn in zip(state, s_decay, k_tail, vn16)]
        o = [t[c_len:] + _dot(aq, vn) for t, aq, vn in zip(ws, a_qk, vn16)]
        o = [t * lax.rsqrt(jnp.mean(t * t, axis=-1, keepdims=True) + NORM_EPS) * out_gain for t in o]
        for (b, h), t in zip(units, o):
            lanes = slice(h * DN_DIM, (h + 1) * DN_DIM)
            o_ref[b, rows, lanes] = (t * z_ref[b, rows, lanes]).astype(o_ref.dtype)
        return new_state

    n_units = len(seqs) * len(heads)
    state = [st_ref[i] for i in range(n_units)]
    pending = prepare(0)
    for j in range(1, ts // c_len):
        nxt = prepare(j)
        state = advance(j - 1, pending, state)
        pending = nxt
    state = advance(ts // c_len - 1, pending, state)
    for i in range(n_units):
        st_ref[i] = state[i]


def _deltanet(qkv, z_gate, ba, gate_params, out_gain, batch, ts=128):
    t = qkv.shape[0]
    seq = t // batch

    def rows(w):
        return pl.BlockSpec((batch, ts, w), lambda s: (0, s, 0))

    out = pl.pallas_call(
        functools.partial(_dn_kernel, ts=ts, batch=batch),
        grid=(seq // ts,),
        in_specs=[rows(3 * DN_WIDTH), rows(DN_WIDTH), rows(LANES),
                  pl.BlockSpec((8, LANES), lambda s: (0, 0)),
                  pl.BlockSpec((1, DN_DIM), lambda s: (0, 0))],
        out_specs=rows(DN_WIDTH),
        out_shape=jax.ShapeDtypeStruct((batch, seq, DN_WIDTH), BF16),
        scratch_shapes=[pltpu.VMEM((batch * DN_HEADS, DN_DIM, DN_DIM), F32)],
        compiler_params=_params("arbitrary"),
        name="deltanet",
    )(qkv.reshape(batch, seq, -1), z_gate.reshape(batch, seq, -1), ba.reshape(batch, seq, -1), gate_params,
      out_gain.reshape(1, DN_DIM))
    return out.reshape(t, DN_WIDTH)


def _swa_bias(slopes, dilation):
    blk = SWA_BLOCK
    shape = (2 * blk, 2 * blk)
    row = lax.broadcasted_iota(jnp.int32, shape, 0)
    col = lax.broadcasted_iota(jnp.int32, shape, 1)
    rel = (row & (blk - 1)) + blk - col
    slope = jnp.where(row < blk, slopes[0] * dilation, slopes[1] * dilation)
    return jnp.where((rel >= 0) & (rel <= blk), -slope * rel.astype(F32), NEG_BIG)


def _swa_kernel(q0_ref, q1_ref, k0_ref, k1_ref, v0_ref, v1_ref, o_ref, kv_scr, acc_scr, lse_scr, *, tq, group):
    blk = SWA_BLOCK
    q_refs = (q0_ref, q1_ref)
    pairs = range(SWA_HEADS // 2)
    n_kv = 2 * len(pairs)
    have_prev = pl.program_id(1) > 0

    @pl.when(jnp.logical_not(have_prev))
    def _():
        kv_scr[:, 0:tq, :] = jnp.zeros((n_kv, tq, LANES), F32)

    for c, ref in enumerate((k0_ref, k1_ref, v0_ref, v1_ref)):
        kv_scr[c, tq:2 * tq, :] = ref[...]

    low = lax.broadcasted_iota(jnp.int32, (blk, LANES), 1) < SWA_DIM
    same_block = lax.broadcasted_iota(jnp.int32, (2 * blk, 2 * blk), 1) >= blk
    slopes = [2.0 ** (-8.0 * (h + 1) / SWA_HEADS) for h in range(SWA_HEADS)]

    def run_group(pi, dilation, bias, starts, prev_valid):
        units = range(len(starts))
        if dilation > 1:
            q_rows = [pl.ds(s, blk, stride=dilation) for s in starts]
            kv_rows = [pl.ds(tq + s - dilation * blk, 2 * blk, stride=dilation) for s in starts]
        else:
            q_rows = [pl.ds(s, blk) for s in starts]
            kv_rows = [pl.ds(tq + s - blk, 2 * blk) for s in starts]
        up = [(u, p) for u in units for p in pairs]
        q = [(q_refs[p][q_rows[u], :] * (SWA_DIM ** -0.5)).astype(BF16) for u, p in up]
        zero = jnp.zeros_like(q[0])
        q2 = [jnp.concatenate([jnp.where(low, t, zero), jnp.where(low, zero, t)], axis=0) for t in q]
        k2 = [kv_scr[p, kv_rows[u], :].astype(BF16) for u, p in up]
        v2 = [kv_scr[len(pairs) + p, kv_rows[u], :].astype(BF16) for u, p in up]

        def unit_bias(u, p):
            if prev_valid[u] is None:
                return bias[p]
            return jnp.where(jnp.logical_or(same_block, prev_valid[u]), bias[p], NEG_BIG)

        s = [_dot_nt(a, b) + unit_bias(u, p) for (u, p), a, b in zip(up, q2, k2)]
        m = [jnp.max(t, axis=-1, keepdims=True) for t in s]
        e = [jnp.exp(t - mm) for t, mm in zip(s, m)]
        den = [jnp.sum(t, axis=-1, keepdims=True) for t in e]
        pv = [_dot(t.astype(BF16), b) for t, b in zip(e, v2)]
        for (u, p), mm, dd, o in zip(up, m, den, pv):
            inv = 1.0 / dd
            lse = mm + jnp.log(dd)
            acc_scr[pi, p, q_rows[u], :] = jnp.where(low, o[:blk] * inv[:blk], o[blk:] * inv[blk:])
            lse_scr[pi, p, q_rows[u], :] = jnp.where(low, lse[:blk], lse[blk:])

    for pi, (_, dilation) in enumerate(SWA_PATTERNS):
        bias = [_swa_bias(slopes[2 * p:2 * p + 2], dilation) for p in pairs]
        per_residue = tq // (dilation * blk)
        if per_residue >= group:
            n_trips = per_residue // group
            for r in range(dilation):
                def trip(g, carry, r=r, pi=pi, dilation=dilation, bias=bias):
                    starts = [r + dilation * blk * (g * group + u) for u in range(group)]
                    first = jnp.logical_or(have_prev, g > 0)
                    run_group(pi, dilation, bias, starts, [first] + [None] * (group - 1))
                    return carry
                if n_trips == 1:
                    trip(0, 0)
                else:
                    lax.fori_loop(0, n_trips, trip, 0)
        else:
            assert per_residue == 1 and dilation % group == 0
            for r0 in range(0, dilation, group):
                run_group(pi, dilation, bias, [r0 + u for u in range(group)], [have_prev] * group)

    step = 256
    n_pat = len(SWA_PATTERNS)
    for c in range(tq // step):
        rows = slice(c * step, (c + 1) * step)
        for p in pairs:
            lses = [lse_scr[pi, p, rows, :] for pi in range(n_pat)]
            top = functools.reduce(jnp.maximum, lses)
            es = [jnp.exp(l - top) for l in lses]
            inv = 1.0 / functools.reduce(lambda a, b: a + b, es)
            merged = functools.reduce(lambda a, b: a + b, [es[pi] * acc_scr[pi, p, rows, :] for pi in range(n_pat)])
            o_ref[rows, p * LANES:(p + 1) * LANES] = (merged * inv).astype(o_ref.dtype)

    kv_scr[:, 0:tq, :] = kv_scr[:, tq:2 * tq, :]


def _swa(swa_qkv, batch, tq=2048, group=4):
    t = swa_qkv.shape[0]
    nj = t // batch // tq
    n_tiles = SWA_WIDTH // LANES
    n_pat = len(SWA_PATTERNS)
    return pl.pallas_call(
        functools.partial(_swa_kernel, tq=tq, group=group),
        grid=(batch, nj),
        in_specs=[pl.BlockSpec((tq, LANES), functools.partial(lambda b, j, c: (b * nj + j, c), c=c))
                  for c in range(3 * n_tiles)],
        out_specs=pl.BlockSpec((tq, SWA_WIDTH), lambda b, j: (b * nj + j, 0)),
        out_shape=jax.ShapeDtypeStruct((t, SWA_WIDTH), BF16),
        scratch_shapes=[pltpu.VMEM((2 * n_tiles, 2 * tq, LANES), F32),
                        pltpu.VMEM((n_pat, n_tiles, tq, LANES), F32),
                        pltpu.VMEM((n_pat, n_tiles, tq, LANES), F32)],
        compiler_params=_params("parallel", "arbitrary"),
        name="swa",
    )(*([swa_qkv] * (3 * n_tiles)))


def _mix_out_kernel(h_ref, a_ref, b_ref, c_ref, w_ref, o_ref):
    y = h_ref[...] + _dot(a_ref[...], w_ref[0:DN_WIDTH, :])
    y = y + _dot(b_ref[...], w_ref[DN_WIDTH:DN_WIDTH + SWA_WIDTH, :])
    y = y + _dot(c_ref[...], w_ref[DN_WIDTH + SWA_WIDTH:, :])
    o_ref[...] = y


def _mix_out(h, out_a, out_b, out_c, w_out, tm=512):
    t, d = h.shape

    def row(w):
        return pl.BlockSpec((tm, w), lambda i: (i, 0))

    return pl.pallas_call(
        _mix_out_kernel,
        grid=(t // tm,),
        in_specs=[row(d), row(DN_WIDTH), row(SWA_WIDTH), row(SG_WIDTH), _resident(w_out.shape)],
        out_specs=row(d),
        out_shape=jax.ShapeDtypeStruct((t, d), F32),
        compiler_params=_params("parallel"),
        name="mix_out",
    )(h, out_a, out_b, out_c, w_out)


def _kv_kernel(mem_ref, g_ref, w_ref, k_ref, v_ref):
    n = _rms(mem_ref[...], g_ref[...]).astype(BF16)
    d = k_ref.shape[-1]
    k_ref[...] = _dot(n, w_ref[:, :d]).astype(BF16)
    v_ref[...] = _dot(n, w_ref[:, d:]).astype(BF16)


def _mem_kv(mem, gain, w_kv):
    b, m, d = mem.shape
    blk = pl.BlockSpec((None, m, d), lambda i: (i, 0, 0))
    return pl.pallas_call(
        _kv_kernel,
        grid=(b,),
        in_specs=[blk, pl.BlockSpec((1, d), lambda i: (0, 0)), _resident((d, 2 * d))],
        out_specs=[blk, blk],
        out_shape=[jax.ShapeDtypeStruct((b, m, d), BF16)] * 2,
        compiler_params=_params("parallel"),
        name="xa_kv",
    )(mem, gain.reshape(1, d), w_kv)


def _xa_kernel(h_ref, g_ref, wq_ref, k_ref, v_ref, wo_ref, o_ref, att_scr):
    x = h_ref[...]
    n = _rms(x, g_ref[...]).astype(BF16)
    d = x.shape[-1]
    dh = d // XA_HEADS
    q = _dot(n, wq_ref[...]).astype(BF16)
    for h in range(XA_HEADS):
        lanes = slice(h * dh, (h + 1) * dh)
        s = _dot_nt(q[:, lanes], k_ref[:, lanes]) * (dh ** -0.5)
        m = jnp.max(s, axis=-1, keepdims=True)
        p = jnp.exp(s - m)
        p = p * (1.0 / jnp.sum(p, axis=-1, keepdims=True))
        att_scr[:, lanes] = _dot(p.astype(BF16), v_ref[:, lanes]).astype(BF16)
    o_ref[...] = x + _dot(att_scr[...], wo_ref[...])


def _cross_attention(h, gain, w_q, k, v, w_o, batch, tm=512):
    t, d = h.shape
    ns = t // batch // tm
    m = k.shape[1]
    row = pl.BlockSpec((tm, d), lambda b, s: (b * ns + s, 0))
    kv = pl.BlockSpec((None, m, d), lambda b, s: (b, 0, 0))
    return pl.pallas_call(
        _xa_kernel,
        grid=(batch, ns),
        in_specs=[row, pl.BlockSpec((1, d), lambda b, s: (0, 0)), _resident((d, d)), kv, kv, _resident((d, d))],
        out_specs=row,
        out_shape=jax.ShapeDtypeStruct((t, d), F32),
        scratch_shapes=[pltpu.VMEM((tm, d), BF16)],
        compiler_params=_params("parallel", "parallel"),
        name="cross_attn",
    )(h, gain.reshape(1, d), w_q, k, v, w_o)


def _mix_in_weights(w_in):
    o_z = 3 * DN_WIDTH
    o_b = o_z + DN_WIDTH
    o_swa = o_b + 2 * DN_HEADS
    o_sg = o_swa + 3 * SWA_WIDTH
    pad = jnp.zeros((w_in.shape[0], LANES - 2 * DN_HEADS), w_in.dtype)
    return jnp.concatenate([w_in[:, :o_b], w_in[:, o_swa:], w_in[:, o_b:o_swa], pad], axis=1).astype(BF16)


def kernel(x, mem, ffn1_norm, ffn1_w_gate_up, ffn1_w_down, mix_norm, mix_w_in, dn_conv_w, dn_a_log, dn_dt_bias,
           dn_out_norm, sg_norm_gain, sg_norm_bias, sg_w_spatial, sg_b_spatial, mix_w_out, xa_norm, xa_mem_norm,
           xa_w_q, xa_w_kv, xa_w_o, ffn2_norm, ffn2_w_gate_up, ffn2_w_down, final_norm):
    batch, seq, d = x.shape
    depth = ffn1_norm.shape[0]
    h = x.reshape(batch * seq, d)
    for i in range(depth):
        h = _ffn(h, ffn1_norm[i], ffn1_w_gate_up[i].astype(BF16), ffn1_w_down[i].astype(BF16))

        gate_params = jnp.zeros((8, LANES), F32)
        gate_params = gate_params.at[0, DN_HEADS:2 * DN_HEADS].set(dn_a_log[i])
        gate_params = gate_params.at[1, DN_HEADS:2 * DN_HEADS].set(dn_dt_bias[i])
        b_sp_rows = jnp.repeat(sg_b_spatial[i].T, SG_DIM, axis=1)
        qkv, z_gate, ba, swa_qkv, out_c = _proj(h, mix_norm[i], _mix_in_weights(mix_w_in[i]), dn_conv_w[i],
                                                sg_norm_gain[i], sg_norm_bias[i], sg_w_spatial[i], b_sp_rows, seq)
        out_a = _deltanet(qkv, z_gate, ba, gate_params, dn_out_norm[i], batch)
        out_b = _swa(swa_qkv, batch)
        h = _mix_out(h, out_a, out_b, out_c, mix_w_out[i].astype(BF16))

        k, v = _mem_kv(mem, xa_mem_norm[i], xa_w_kv[i].astype(BF16))
        h = _cross_attention(h, xa_norm[i], xa_w_q[i].astype(BF16), k, v, xa_w_o[i].astype(BF16), batch)

        last = i == depth - 1
        h = _ffn(h, ffn2_norm[i], ffn2_w_gate_up[i].astype(BF16), ffn2_w_down[i].astype(BF16),
                 final_gain=final_norm if last else None)
    return h.reshape(batch, seq, d)
```

```python
import functools

import jax
import jax.numpy as jnp
from jax import lax
from jax.experimental import pallas as pl
from jax.experimental.pallas import tpu as pltpu

F32 = jnp.float32
BF16 = jnp.bfloat16
NORM_EPS = 1e-6
LANES = 128
VMEM_LIMIT = 56 * 1024 * 1024

DN_HEADS = 4
DN_DIM = 128
DN_WIDTH = DN_HEADS * DN_DIM
DN_CONV = 4
DN_CHUNK = 64
SWA_HEADS = 4
SWA_DIM = 64
SWA_WIDTH = SWA_HEADS * SWA_DIM
SWA_PATTERNS = ((128, 1), (512, 4), (2048, 16))
SWA_BLOCK = 128
SG_GROUPS = 4
SG_DIM = 64
SG_WIDTH = SG_GROUPS * SG_DIM
SG_CHUNK = 128
XA_HEADS = 4
NEG_BIG = -1e30


def _resident(shape):
    return pl.BlockSpec(shape, lambda *_: (0,) * len(shape), pipeline_mode=pl.Buffered(1))


def _params(*sem):
    return pltpu.CompilerParams(dimension_semantics=sem, vmem_limit_bytes=VMEM_LIMIT)


def _rms(x, gain):
    ms = jnp.mean(x * x, axis=-1, keepdims=True)
    return x * lax.rsqrt(ms + NORM_EPS) * gain


def _dot(a, b):
    return jnp.dot(a, b, preferred_element_type=F32)


def _dot_nt(a, b):
    return lax.dot_general(a, b, (((1,), (1,)), ((), ())), preferred_element_type=F32)


def _dot_tn(a, b):
    return lax.dot_general(a, b, (((0,), (0,)), ((), ())), preferred_element_type=F32)


def _silu(x):
    return x * jax.nn.sigmoid(x)


def _ffn_kernel(*refs, d_ff, chunk, final):
    if final:
        x_ref, g_ref, wgu_ref, wd_ref, fg_ref, o_ref, h_scr = refs
    else:
        x_ref, g_ref, wgu_ref, wd_ref, o_ref, h_scr = refs
    x = x_ref[...]
    n = _rms(x, g_ref[...]).astype(BF16)
    for c in range(d_ff // chunk):
        gate = _dot(n, wgu_ref[:, c * chunk:(c + 1) * chunk])
        up = _dot(n, wgu_ref[:, d_ff + c * chunk:d_ff + (c + 1) * chunk])
        h_scr[:, c * chunk:(c + 1) * chunk] = (_silu(gate) * up).astype(BF16)
    y = x + 0.5 * _dot(h_scr[...], wd_ref[...])
    if final:
        y = _rms(y, fg_ref[...])
    o_ref[...] = y


def _ffn(h, gain, w_gu, w_d, final_gain=None, tm=512, chunk=256):
    t, d = h.shape
    d_ff = w_d.shape[0]
    final = final_gain is not None
    row = pl.BlockSpec((tm, d), lambda i: (i, 0))
    vec = pl.BlockSpec((1, d), lambda i: (0, 0))
    in_specs = [row, vec, _resident((d, 2 * d_ff)), _resident((d_ff, d))]
    args = [h, gain.reshape(1, d), w_gu, w_d]
    if final:
        in_specs.append(vec)
        args.append(final_gain.reshape(1, d))
    return pl.pallas_call(
        functools.partial(_ffn_kernel, d_ff=d_ff, chunk=chunk, final=final),
        grid=(t // tm,),
        in_specs=in_specs,
        out_specs=row,
        out_shape=jax.ShapeDtypeStruct((t, d), F32),
        scratch_shapes=[pltpu.VMEM((tm, d_ff), BF16)],
        compiler_params=_params("parallel"),
        name="ffn_final" if final else "ffn",
    )(*args)


def _proj_kernel(x_ref, g_ref, w_ref, cw_ref, lg_ref, lb_ref, wsp_ref, bsp_ref,
                 qkv_ref, z_ref, ba_ref, swa_ref, c_ref, xs_ref, *, tm, tiles_per_seq):
    halo = 8
    piece = DN_CHUNK
    n = _rms(x_ref[...], g_ref[...]).astype(BF16)
    o_z = 3 * DN_WIDTH
    o_swa = o_z + DN_WIDTH
    o_sg = o_swa + 3 * SWA_WIDTH
    o_ba = o_sg + 2 * SG_WIDTH

    @pl.when(pl.program_id(0) % tiles_per_seq == 0)
    def _():
        xs_ref[0:halo, :] = jnp.zeros((halo, o_z), F32)

    cw = cw_ref[...]

    def stage(part):
        cols = slice(part * DN_WIDTH, (part + 1) * DN_WIDTH)
        xs_ref[halo:halo + tm, cols] = _dot(n, w_ref[:, cols])

    def dn_front(part):
        cols = slice(part * DN_WIDTH, (part + 1) * DN_WIDTH)
        for c in range(tm // piece):
            x = xs_ref[c * piece:c * piece + piece + halo, cols]
            acc = cw[DN_CONV - 1:DN_CONV, cols] * x[halo:halo + piece]
            for j in range(DN_CONV - 1):
                lo = halo - (DN_CONV - 1) + j
                acc = acc + cw[j:j + 1, cols] * x[lo:lo + piece]
            a = _silu(acc)
            rows = slice(c * piece, (c + 1) * piece)
            if part == 2:
                qkv_ref[rows, cols] = a
                continue
            for h in range(DN_HEADS):
                y = a[:, h * DN_DIM:(h + 1) * DN_DIM]
                inv = lax.rsqrt(jnp.sum(y * y, axis=-1, keepdims=True) + NORM_EPS)
                if part == 0:
                    inv = inv * (DN_DIM ** -0.5)
                qkv_ref[rows, part * DN_WIDTH + h * DN_DIM:part * DN_WIDTH + (h + 1) * DN_DIM] = y * inv

    stage(0)
    stage(1)
    dn_front(0)
    stage(2)
    dn_front(1)
    z_ref[...] = _silu(_dot(n, w_ref[:, o_z:o_swa]))
    swa_ref[...] = _dot(n, w_ref[:, o_swa:o_sg])
    dn_front(2)
    ba_ref[...] = _dot(n, w_ref[:, o_ba:o_ba + LANES])

    uv = jax.nn.gelu(_dot(n, w_ref[:, o_sg:o_ba]))
    u = uv[:, :SG_WIDTH]
    v = uv[:, SG_WIDTH:]
    mu = jnp.mean(v, axis=-1, keepdims=True)
    vc = v - mu
    var = jnp.mean(vc * vc, axis=-1, keepdims=True)
    vn = vc * lax.rsqrt(var + NORM_EPS) * lg_ref[...] + lb_ref[...]
    row = lax.broadcasted_iota(jnp.int32, (SG_CHUNK, SG_CHUNK), 0)
    col = lax.broadcasted_iota(jnp.int32, (SG_CHUNK, SG_CHUNK), 1)
    causal = row >= col
    low_lanes = col < SG_DIM
    w_causal = [jnp.where(causal, wsp_ref[g], 0.0).astype(BF16) for g in range(SG_GROUPS)]
    for c in range(tm // SG_CHUNK):
        rows = slice(c * SG_CHUNK, (c + 1) * SG_CHUNK)
        for p in range(SG_GROUPS // 2):
            lanes = slice(p * LANES, (p + 1) * LANES)
            vp = vn[rows, lanes]
            v_lo = jnp.where(low_lanes, vp, 0.0).astype(BF16)
            v_hi = jnp.where(low_lanes, 0.0, vp).astype(BF16)
            mixed = _dot(w_causal[2 * p], v_lo) + _dot(w_causal[2 * p + 1], v_hi) + bsp_ref[:, lanes]
            c_ref[rows, lanes] = (u[rows, lanes] * mixed).astype(BF16)

    xs_ref[0:halo, :] = xs_ref[tm:tm + halo, :]


def _proj(h, gain, w_all, conv_w, sg_gain, sg_bias, w_sp, b_sp_rows, seq, tm=512):
    t, d = h.shape
    ncols = w_all.shape[1]

    def row(w):
        return pl.BlockSpec((tm, w), lambda i: (i, 0))

    return pl.pallas_call(
        functools.partial(_proj_kernel, tm=tm, tiles_per_seq=seq // tm),
        grid=(t // tm,),
        in_specs=[row(d), pl.BlockSpec((1, d), lambda i: (0, 0)), _resident((d, ncols)),
                  pl.BlockSpec((DN_CONV, 3 * DN_WIDTH), lambda i: (0, 0)),
                  pl.BlockSpec((1, SG_WIDTH), lambda i: (0, 0)),
                  pl.BlockSpec((1, SG_WIDTH), lambda i: (0, 0)),
                  _resident((SG_GROUPS, SG_CHUNK, SG_CHUNK)),
                  _resident((SG_CHUNK, SG_WIDTH))],
        out_specs=[row(3 * DN_WIDTH), row(DN_WIDTH), row(LANES), row(3 * SWA_WIDTH), row(SG_WIDTH)],
        out_shape=[jax.ShapeDtypeStruct((t, 3 * DN_WIDTH), F32),
                   jax.ShapeDtypeStruct((t, DN_WIDTH), F32),
                   jax.ShapeDtypeStruct((t, LANES), F32),
                   jax.ShapeDtypeStruct((t, 3 * SWA_WIDTH), F32),
                   jax.ShapeDtypeStruct((t, SG_WIDTH), BF16)],
        scratch_shapes=[pltpu.VMEM((tm + 8, 3 * DN_WIDTH), F32)],
        compiler_params=_params("arbitrary"),
        name="mix_proj",
    )(h, gain.reshape(1, d), w_all, conv_w, sg_gain.reshape(1, -1), sg_bias.reshape(1, -1), w_sp, b_sp_rows)


def _split3_bf16(a):
    hi = a.astype(BF16)
    r1 = a - hi.astype(F32)
    mid = r1.astype(BF16)
    lo = (r1 - mid.astype(F32)).astype(BF16)
    return hi, mid, lo


def _dot_exact_lhs(lhs16, rhs):
    hi, mid, lo = _split3_bf16(rhs)
    return _dot(lhs16, hi) + _dot(lhs16, mid) + _dot(lhs16, lo)


def _dn_kernel(qkv_ref, z_ref, ba_ref, gp_ref, og_ref, o_ref, st_ref, *, ts, batch):
    c_len = DN_CHUNK
    heads = range(DN_HEADS)
    seqs = range(batch)

    @pl.when(pl.program_id(0) == 0)
    def _():
        st_ref[...] = jnp.zeros_like(st_ref)

    neg_a = -jnp.exp(gp_ref[0:1, :])
    dt_bias = gp_ref[1:2, :]
    out_gain = og_ref[...]

    row = lax.broadcasted_iota(jnp.int32, (c_len, c_len), 0)
    col = lax.broadcasted_iota(jnp.int32, (c_len, c_len), 1)
    tri_incl = row >= col
    tri_strict = row > col
    eye = (row == col).astype(F32)
    tri16 = tri_incl.astype(BF16)
    ones16 = jnp.ones((c_len, c_len), BF16)
    base_mask = ((row >> 1) == (col >> 1)).astype(F32)
    level_masks = []
    sh = 1
    while (1 << sh) < c_len:
        level_masks.append((((row >> (sh + 1)) == (col >> (sh + 1))) & ((row >> sh) != (col >> sh))).astype(F32))
        sh += 1
    rowh = lax.broadcasted_iota(jnp.int32, (c_len, DN_HEADS * LANES), 0)
    colh = lax.broadcasted_iota(jnp.int32, (c_len, DN_HEADS * LANES), 1) & (LANES - 1)
    upto = (rowh <= colh).astype(F32)

    def prepare(j):
        rows = slice(j * c_len, (j + 1) * c_len)
        units = [(b, h) for b in seqs for h in heads]
        beta_all, gcol, grow = [], [], []
        for b in seqs:
            ba = ba_ref[b, rows, :]
            beta_all.append(jax.nn.sigmoid(ba))
            g_all = neg_a * jax.nn.softplus(ba + dt_bias)
            gseg = jnp.concatenate(
                [jnp.broadcast_to(g_all[:, DN_HEADS + h:DN_HEADS + h + 1], (c_len, LANES)) for h in heads], axis=1)
            gcol.append(_dot_exact_lhs(tri16, gseg))
            grow.append(_dot_exact_lhs(ones16, gseg * upto))

        def head_lanes(b, base, h):
            return qkv_ref[b, rows, base + h * DN_DIM:base + (h + 1) * DN_DIM]

        q = [head_lanes(b, 0, h) for b, h in units]
        k = [head_lanes(b, DN_WIDTH, h) for b, h in units]
        v = [head_lanes(b, 2 * DN_WIDTH, h) for b, h in units]
        beta = [beta_all[b][:, h:h + 1] for b, h in units]
        gcb = [gcol[b][:, h * LANES:(h + 1) * LANES] for b, h in units]
        grow_h = [grow[b][:, h * LANES:(h + 1) * LANES] for b, h in units]
        g_last = [t[:, c_len:c_len + 1] for t in grow_h]
        decay = [jnp.where(tri_incl, jnp.exp(jnp.where(tri_incl, gc[:, :c_len] - gr[:, :c_len], 0.0)), 0.0)
                 for gc, gr in zip(gcb, grow_h)]
        kb = [t * bb for t, bb in zip(k, beta)]
        k16 = [t.astype(BF16) for t in k]
        both = [_dot_nt(jnp.concatenate([t, qq], axis=0).astype(BF16), t16) for t, qq, t16 in zip(kb, q, k16)]
        lower = [jnp.where(tri_strict, t[:c_len] * d, 0.0) for t, d in zip(both, decay)]
        a_qk = [jnp.where(tri_incl, t[c_len:] * d, 0.0).astype(BF16) for t, d in zip(both, decay)]
        t_inv = [eye - lo * base_mask for lo in lower]
        for m in level_masks:
            t16 = [t.astype(BF16) for t in t_inv]
            left = [_dot(t, (lo * m).astype(BF16)).astype(BF16) for t, lo in zip(t16, lower)]
            t_inv = [t - _dot(l, t_b) for t, l, t_b in zip(t_inv, left, t16)]
        t16 = [t.astype(BF16) for t in t_inv]
        egc = [jnp.exp(t) for t in gcb]
        uw = [_dot(t, jnp.concatenate([vv * bb, kk * e], axis=1).astype(BF16))
              for t, vv, bb, kk, e in zip(t16, v, beta, kb, egc)]
        u = [t[:, :DN_DIM] for t in uw]
        k_tail = [(kk * jnp.exp(gl - gc)).astype(BF16) for kk, gl, gc in zip(k, g_last, gcb)]
        wq = [jnp.concatenate([t[:, DN_DIM:], qq * e], axis=0).astype(BF16) for t, qq, e in zip(uw, q, egc)]
        s_decay = [jnp.exp(gl[0:1, :]) for gl in g_last]
        return u, wq, a_qk, k_tail, s_decay

    def advance(j, prepared, state):
        rows = slice(j * c_len, (j + 1) * c_len)
        units = [(b, h) for b in seqs for h in heads]
        u, wq, a_qk, k_tail, s_decay = prepared
        ws = [_dot(t, s.astype(BF16)) for t, s in zip(wq, state)]
        vn16 = [(uu - t[:c_len]).astype(BF16) for uu, t in zip(u, ws)]
        new_state = [s * d + _dot_tn(kt, vn) for s, d, kt, vn in zip(state, s_decay, k_tail, vn16)]
        o = [t[c_len:] + _dot(aq, vn) for t, aq, vn in zip(ws, a_qk, vn16)]
        o = [t * lax.rsqrt(jnp.mean(t * t, axis=-1, keepdims=True) + NORM_EPS) * out_gain for t in o]
        for (b, h), t in zip(units, o):
            lanes = slice(h * DN_DIM, (h + 1) * DN_DIM)
            o_ref[b, rows, lanes] = (t * z_ref[b, rows, lanes]).astype(o_ref.dtype)
        return new_state

    n_units = len(seqs) * len(heads)
    state = [st_ref[i] for i in range(n_units)]
    pending = prepare(0)
    for j in range(1, ts // c_len):
        nxt = prepare(j)
        state = advance(j - 1, pending, state)
        pending = nxt
    state = advance(ts // c_len - 1, pending, state)
    for i in range(n_units):
        st_ref[i] = state[i]


def _deltanet(qkv, z_gate, ba, gate_params, out_gain, batch, ts=256):
    t = qkv.shape[0]
    seq = t // batch

    def rows(w):
        return pl.BlockSpec((batch, ts, w), lambda s: (0, s, 0))

    out = pl.pallas_call(
        functools.partial(_dn_kernel, ts=ts, batch=batch),
        grid=(seq // ts,),
        in_specs=[rows(3 * DN_WIDTH), rows(DN_WIDTH), rows(LANES),
                  pl.BlockSpec((8, LANES), lambda s: (0, 0)),
                  pl.BlockSpec((1, DN_DIM), lambda s: (0, 0))],
        out_specs=rows(DN_WIDTH),
        out_shape=jax.ShapeDtypeStruct((batch, seq, DN_WIDTH), BF16),
        scratch_shapes=[pltpu.VMEM((batch * DN_HEADS, DN_DIM, DN_DIM), F32)],
        compiler_params=_params("arbitrary"),
        name="deltanet",
    )(qkv.reshape(batch, seq, -1), z_gate.reshape(batch, seq, -1), ba.reshape(batch, seq, -1), gate_params,
      out_gain.reshape(1, DN_DIM))
    return out.reshape(t, DN_WIDTH)


def _swa_bias(slopes, dilation):
    blk = SWA_BLOCK
    shape = (2 * blk, 2 * blk)
    row = lax.broadcasted_iota(jnp.int32, shape, 0)
    col = lax.broadcasted_iota(jnp.int32, shape, 1)
    rel = (row & (blk - 1)) + blk - col
    slope = jnp.where(row < blk, slopes[0] * dilation, slopes[1] * dilation)
    return jnp.where((rel >= 0) & (rel <= blk), -slope * rel.astype(F32), NEG_BIG)


def _swa_kernel(q0_ref, q1_ref, k0_ref, k1_ref, v0_ref, v1_ref, o_ref, kv_scr, acc_scr, lse_scr, *, tq, group):
    blk = SWA_BLOCK
    q_refs = (q0_ref, q1_ref)
    pairs = range(SWA_HEADS // 2)
    n_kv = 2 * len(pairs)
    have_prev = pl.program_id(1) > 0

    @pl.when(jnp.logical_not(have_prev))
    def _():
        kv_scr[:, 0:tq, :] = jnp.zeros((n_kv, tq, LANES), F32)

    for c, ref in enumerate((k0_ref, k1_ref, v0_ref, v1_ref)):
        kv_scr[c, tq:2 * tq, :] = ref[...]

    low = lax.broadcasted_iota(jnp.int32, (blk, LANES), 1) < SWA_DIM
    same_block = lax.broadcasted_iota(jnp.int32, (2 * blk, 2 * blk), 1) >= blk
    slopes = [2.0 ** (-8.0 * (h + 1) / SWA_HEADS) for h in range(SWA_HEADS)]

    def run_group(pi, dilation, bias, starts, prev_valid):
        units = range(len(starts))
        if dilation > 1:
            q_rows = [pl.ds(s, blk, stride=dilation) for s in starts]
            kv_rows = [pl.ds(tq + s - dilation * blk, 2 * blk, stride=dilation) for s in starts]
        else:
            q_rows = [pl.ds(s, blk) for s in starts]
            kv_rows = [pl.ds(tq + s - blk, 2 * blk) for s in starts]
        up = [(u, p) for u in units for p in pairs]
        q = [(q_refs[p][q_rows[u], :] * (SWA_DIM ** -0.5)).astype(BF16) for u, p in up]
        zero = jnp.zeros_like(q[0])
        q2 = [jnp.concatenate([jnp.where(low, t, zero), jnp.where(low, zero, t)], axis=0) for t in q]
        k2 = [kv_scr[p, kv_rows[u], :].astype(BF16) for u, p in up]
        v2 = [kv_scr[len(pairs) + p, kv_rows[u], :].astype(BF16) for u, p in up]

        def unit_bias(u, p):
            if prev_valid[u] is None:
                return bias[p]
            return jnp.where(jnp.logical_or(same_block, prev_valid[u]), bias[p], NEG_BIG)

        s = [_dot_nt(a, b) + unit_bias(u, p) for (u, p), a, b in zip(up, q2, k2)]
        m = [jnp.max(t, axis=-1, keepdims=True) for t in s]
        e = [jnp.exp(t - mm) for t, mm in zip(s, m)]
        den = [jnp.sum(t, axis=-1, keepdims=True) for t in e]
        pv = [_dot(t.astype(BF16), b) for t, b in zip(e, v2)]
        for (u, p), mm, dd, o in zip(up, m, den, pv):
            inv = 1.0 / dd
            lse = mm + jnp.log(dd)
            acc_scr[pi, p, q_rows[u], :] = jnp.where(low, o[:blk] * inv[:blk], o[blk:] * inv[blk:])
            lse_scr[pi, p, q_rows[u], :] = jnp.where(low, lse[:blk], lse[blk:])

    for pi, (_, dilation) in enumerate(SWA_PATTERNS):
        bias = [_swa_bias(slopes[2 * p:2 * p + 2], dilation) for p in pairs]
        per_residue = tq // (dilation * blk)
        if per_residue >= group:
            n_trips = per_residue // group
            for r in range(dilation):
                def trip(g, carry, r=r, pi=pi, dilation=dilation, bias=bias):
                    starts = [r + dilation * blk * (g * group + u) for u in range(group)]
                    first = jnp.logical_or(have_prev, g > 0)
                    run_group(pi, dilation, bias, starts, [first] + [None] * (group - 1))
                    return carry
                if n_trips == 1:
                    trip(0, 0)
                else:
                    lax.fori_loop(0, n_trips, trip, 0)
        else:
            assert per_residue == 1 and dilation % group == 0
            for r0 in range(0, dilation, group):
                run_group(pi, dilation, bias, [r0 + u for u in range(group)], [have_prev] * group)

    step = 256
    n_pat = len(SWA_PATTERNS)
    for c in range(tq // step):
        rows = slice(c * step, (c + 1) * step)
        for p in pairs:
            lses = [lse_scr[pi, p, rows, :] for pi in range(n_pat)]
            top = functools.reduce(jnp.maximum, lses)
            es = [jnp.exp(l - top) for l in lses]
            inv = 1.0 / functools.reduce(lambda a, b: a + b, es)
            merged = functools.reduce(lambda a, b: a + b, [es[pi] * acc_scr[pi, p, rows, :] for pi in range(n_pat)])
            o_ref[rows, p * LANES:(p + 1) * LANES] = (merged * inv).astype(o_ref.dtype)

    kv_scr[:, 0:tq, :] = kv_scr[:, tq:2 * tq, :]


def _swa(swa_qkv, batch, tq=2048, group=4):
    t = swa_qkv.shape[0]
    nj = t // batch // tq
    n_tiles = SWA_WIDTH // LANES
    n_pat = len(SWA_PATTERNS)
    return pl.pallas_call(
        functools.partial(_swa_kernel, tq=tq, group=group),
        grid=(batch, nj),
        in_specs=[pl.BlockSpec((tq, LANES), functools.partial(lambda b, j, c: (b * nj + j, c), c=c))
                  for c in range(3 * n_tiles)],
        out_specs=pl.BlockSpec((tq, SWA_WIDTH), lambda b, j: (b * nj + j, 0)),
        out_shape=jax.ShapeDtypeStruct((t, SWA_WIDTH), BF16),
        scratch_shapes=[pltpu.VMEM((2 * n_tiles, 2 * tq, LANES), F32),
                        pltpu.VMEM((n_pat, n_tiles, tq, LANES), F32),
                        pltpu.VMEM((n_pat, n_tiles, tq, LANES), F32)],
        compiler_params=_params("parallel", "arbitrary"),
        name="swa",
    )(*([swa_qkv] * (3 * n_tiles)))


def _kv_kernel(mem_ref, g_ref, w_ref, k_ref, v_ref):
    n = _rms(mem_ref[...], g_ref[...]).astype(BF16)
    d = k_ref.shape[-1]
    k_ref[...] = _dot(n, w_ref[:, :d]).astype(BF16)
    v_ref[...] = _dot(n, w_ref[:, d:]).astype(BF16)


def _mem_kv(mem, gain, w_kv):
    b, m, d = mem.shape
    blk = pl.BlockSpec((None, m, d), lambda i: (i, 0, 0))
    return pl.pallas_call(
        _kv_kernel,
        grid=(b,),
        in_specs=[blk, pl.BlockSpec((1, d), lambda i: (0, 0)), _resident((d, 2 * d))],
        out_specs=[blk, blk],
        out_shape=[jax.ShapeDtypeStruct((b, m, d), BF16)] * 2,
        compiler_params=_params("parallel"),
        name="xa_kv",
    )(mem, gain.reshape(1, d), w_kv)


def _mix_xa_kernel(h_ref, a_ref, b_ref, c_ref, wm_ref, g_ref, wq_ref, k_ref, v_ref, wo_ref, o_ref, att_scr):
    x = h_ref[...] + _dot(a_ref[...], wm_ref[0:DN_WIDTH, :])
    x = x + _dot(b_ref[...], wm_ref[DN_WIDTH:DN_WIDTH + SWA_WIDTH, :])
    x = x + _dot(c_ref[...], wm_ref[DN_WIDTH + SWA_WIDTH:, :])
    n = _rms(x, g_ref[...]).astype(BF16)
    d = x.shape[-1]
    dh = d // XA_HEADS
    q = _dot(n, wq_ref[...]).astype(BF16)
    lanes = [slice(h * dh, (h + 1) * dh) for h in range(XA_HEADS)]
    s = [_dot_nt(q[:, ln], k_ref[:, ln]) * (dh ** -0.5) for ln in lanes]
    m = [jnp.max(t, axis=-1, keepdims=True) for t in s]
    p = [jnp.exp(t - mm) for t, mm in zip(s, m)]
    p = [t * (1.0 / jnp.sum(t, axis=-1, keepdims=True)) for t in p]
    for t, ln in zip(p, lanes):
        att_scr[:, ln] = _dot(t.astype(BF16), v_ref[:, ln]).astype(BF16)
    o_ref[...] = x + _dot(att_scr[...], wo_ref[...])


def _mix_out_cross_attention(h, out_a, out_b, out_c, w_out, gain, w_q, k, v, w_o, batch, tm=512):
    t, d = h.shape
    ns = t // batch // tm
    m = k.shape[1]

    def row(w):
        return pl.BlockSpec((tm, w), lambda b, s: (b * ns + s, 0))

    kv = pl.BlockSpec((None, m, d), lambda b, s: (b, 0, 0))
    return pl.pallas_call(
        _mix_xa_kernel,
        grid=(batch, ns),
        in_specs=[row(d), row(DN_WIDTH), row(SWA_WIDTH), row(SG_WIDTH), _resident(w_out.shape),
                  pl.BlockSpec((1, d), lambda b, s: (0, 0)), _resident((d, d)), kv, kv, _resident((d, d))],
        out_specs=row(d),
        out_shape=jax.ShapeDtypeStruct((t, d), F32),
        scratch_shapes=[pltpu.VMEM((tm, d), BF16)],
        compiler_params=_params("parallel", "parallel"),
        name="mix_out_cross_attn",
    )(h, out_a, out_b, out_c, w_out, gain.reshape(1, d), w_q, k, v, w_o)


def _mix_in_weights(w_in):
    o_z = 3 * DN_WIDTH
    o_b = o_z + DN_WIDTH
    o_swa = o_b + 2 * DN_HEADS
    o_sg = o_swa + 3 * SWA_WIDTH
    pad = jnp.zeros((w_in.shape[0], LANES - 2 * DN_HEADS), w_in.dtype)
    return jnp.concatenate([w_in[:, :o_b], w_in[:, o_swa:], w_in[:, o_b:o_swa], pad], axis=1).astype(BF16)


def kernel(x, mem, ffn1_norm, ffn1_w_gate_up, ffn1_w_down, mix_norm, mix_w_in, dn_conv_w, dn_a_log, dn_dt_bias,
           dn_out_norm, sg_norm_gain, sg_norm_bias, sg_w_spatial, sg_b_spatial, mix_w_out, xa_norm, xa_mem_norm,
           xa_w_q, xa_w_kv, xa_w_o, ffn2_norm, ffn2_w_gate_up, ffn2_w_down, final_norm):
    batch, seq, d = x.shape
    depth = ffn1_norm.shape[0]
    h = x.reshape(batch * seq, d)
    for i in range(depth):
        h = _ffn(h, ffn1_norm[i], ffn1_w_gate_up[i].astype(BF16), ffn1_w_down[i].astype(BF16))

        gate_params = jnp.zeros((8, LANES), F32)
        gate_params = gate_params.at[0, DN_HEADS:2 * DN_HEADS].set(dn_a_log[i])
        gate_params = gate_params.at[1, DN_HEADS:2 * DN_HEADS].set(dn_dt_bias[i])
        b_sp_rows = jnp.repeat(sg_b_spatial[i].T, SG_DIM, axis=1)
        qkv, z_gate, ba, swa_qkv, out_c = _proj(h, mix_norm[i], _mix_in_weights(mix_w_in[i]), dn_conv_w[i],
                                                sg_norm_gain[i], sg_norm_bias[i], sg_w_spatial[i], b_sp_rows, seq)
        out_a = _deltanet(qkv, z_gate, ba, gate_params, dn_out_norm[i], batch)
        out_b = _swa(swa_qkv, batch)
        k, v = _mem_kv(mem, xa_mem_norm[i], xa_w_kv[i].astype(BF16))
        h = _mix_out_cross_attention(h, out_a, out_b, out_c, mix_w_out[i].astype(BF16), xa_norm[i],
                                     xa_w_q[i].astype(BF16), k, v, xa_w_o[i].astype(BF16), batch)

        last = i == depth - 1
        h = _ffn(h, ffn2_norm[i], ffn2_w_gate_up[i].astype(BF16), ffn2_w_down[i].astype(BF16),
                 final_gain=final_norm if last else None)
    return h.reshape(batch, seq, d)
```

```python
import functools

import jax
import jax.numpy as jnp
from jax import lax
from jax.experimental import pallas as pl
from jax.experimental.pallas import tpu as pltpu

F32 = jnp.float32
BF16 = jnp.bfloat16
NORM_EPS = 1e-6
LANES = 128
VMEM_LIMIT = 56 * 1024 * 1024

DN_HEADS = 4
DN_DIM = 128
DN_WIDTH = DN_HEADS * DN_DIM
DN_CONV = 4
DN_CHUNK = 64
SWA_HEADS = 4
SWA_DIM = 64
SWA_WIDTH = SWA_HEADS * SWA_DIM
SWA_PATTERNS = ((128, 1), (512, 4), (2048, 16))
SWA_BLOCK = 128
SG_GROUPS = 4
SG_DIM = 64
SG_WIDTH = SG_GROUPS * SG_DIM
SG_CHUNK = 128
XA_HEADS = 4
NEG_BIG = -1e30


def _resident(shape):
    return pl.BlockSpec(shape, lambda *_: (0,) * len(shape), pipeline_mode=pl.Buffered(1))


def _layer_spec(stacked, layer):
    return pl.BlockSpec((None,) + stacked.shape[1:], lambda *_: (layer, 0, 0), pipeline_mode=pl.Buffered(1))


def _params(*sem):
    return pltpu.CompilerParams(dimension_semantics=sem, vmem_limit_bytes=VMEM_LIMIT)


def _rms(x, gain):
    ms = jnp.mean(x * x, axis=-1, keepdims=True)
    return x * lax.rsqrt(ms + NORM_EPS) * gain


def _dot(a, b):
    return jnp.dot(a, b, preferred_element_type=F32)


def _dot_nt(a, b):
    return lax.dot_general(a, b, (((1,), (1,)), ((), ())), preferred_element_type=F32)


def _dot_tn(a, b):
    return lax.dot_general(a, b, (((0,), (0,)), ((), ())), preferred_element_type=F32)


def _silu(x):
    return x * jax.nn.sigmoid(x)


def _cast_kernel(x_ref, o_ref):
    o_ref[...] = x_ref[...].astype(o_ref.dtype)


def _to_bf16(w, tr=256):
    depth, rows, cols = w.shape
    blk = pl.BlockSpec((None, tr, cols), lambda l, r: (l, r, 0))
    return pl.pallas_call(
        _cast_kernel,
        grid=(depth, rows // tr),
        in_specs=[blk],
        out_specs=blk,
        out_shape=jax.ShapeDtypeStruct(w.shape, BF16),
        compiler_params=_params("parallel", "parallel"),
        name="cast_bf16",
    )(w)


MIX_COLS = 3 * DN_WIDTH + DN_WIDTH + 3 * SWA_WIDTH + 2 * SG_WIDTH + LANES


def _cast_mix_in_kernel(x_ref, o_ref):
    o_b = 4 * DN_WIDTH
    o_swa = o_b + 2 * DN_HEADS
    x = x_ref[...]
    rest = x.shape[1] - o_swa
    o_ref[:, 0:o_b] = x[:, 0:o_b].astype(BF16)
    o_ref[:, o_b:o_b + rest] = x[:, o_swa:].astype(BF16)
    gates = jnp.concatenate([x[:, o_b:o_swa], jnp.zeros((x.shape[0], LANES - 2 * DN_HEADS), F32)], axis=1)
    o_ref[:, o_b + rest:] = gates.astype(BF16)


def _mix_in_weights(w_in, tr=256):
    depth, rows, cols = w_in.shape
    return pl.pallas_call(
        _cast_mix_in_kernel,
        grid=(depth, rows // tr),
        in_specs=[pl.BlockSpec((None, tr, cols), lambda l, r: (l, r, 0))],
        out_specs=pl.BlockSpec((None, tr, MIX_COLS), lambda l, r: (l, r, 0)),
        out_shape=jax.ShapeDtypeStruct((depth, rows, MIX_COLS), BF16),
        compiler_params=_params("parallel", "parallel"),
        name="cast_mix_in",
    )(w_in)


def _ffn_kernel(*refs, d_ff, chunk, final):
    if final:
        x_ref, g_ref, wgu_ref, wd_ref, fg_ref, o_ref, h_scr = refs
    else:
        x_ref, g_ref, wgu_ref, wd_ref, o_ref, h_scr = refs
    x = x_ref[...]
    n = _rms(x, g_ref[...]).astype(BF16)
    for c in range(d_ff // chunk):
        gate = _dot(n, wgu_ref[:, c * chunk:(c + 1) * chunk])
        up = _dot(n, wgu_ref[:, d_ff + c * chunk:d_ff + (c + 1) * chunk])
        h_scr[:, c * chunk:(c + 1) * chunk] = (_silu(gate) * up).astype(BF16)
    y = x + 0.5 * _dot(h_scr[...], wd_ref[...])
    if final:
        y = _rms(y, fg_ref[...])
    o_ref[...] = y


def _ffn(h, gain, w_gu, w_d, layer, final_gain=None, tm=1024, chunk=256):
    t, d = h.shape
    d_ff = w_d.shape[1]
    final = final_gain is not None
    row = pl.BlockSpec((tm, d), lambda i: (i, 0))
    vec = pl.BlockSpec((1, d), lambda i: (0, 0))
    in_specs = [row, vec, _layer_spec(w_gu, layer), _layer_spec(w_d, layer)]
    args = [h, gain.reshape(1, d), w_gu, w_d]
    if final:
        in_specs.append(vec)
        args.append(final_gain.reshape(1, d))
    return pl.pallas_call(
        functools.partial(_ffn_kernel, d_ff=d_ff, chunk=chunk, final=final),
        grid=(t // tm,),
        in_specs=in_specs,
        out_specs=row,
        out_shape=jax.ShapeDtypeStruct((t, d), F32),
        scratch_shapes=[pltpu.VMEM((tm, d_ff), BF16)],
        compiler_params=_params("parallel"),
        name="ffn_final" if final else "ffn",
    )(*args)


def _proj_kernel(x_ref, g_ref, w_ref, cw_ref, lg_ref, lb_ref, wsp_ref, bsp_ref,
                 qkv_ref, z_ref, ba_ref, swa_ref, c_ref, xs_ref, *, tm, tiles_per_seq):
    halo = 8
    piece = DN_CHUNK
    n = _rms(x_ref[...], g_ref[...]).astype(BF16)
    o_z = 3 * DN_WIDTH
    o_swa = o_z + DN_WIDTH
    o_sg = o_swa + 3 * SWA_WIDTH
    o_ba = o_sg + 2 * SG_WIDTH

    @pl.when(pl.program_id(0) % tiles_per_seq == 0)
    def _():
        xs_ref[0:halo, :] = jnp.zeros((halo, o_z), F32)

    cw = cw_ref[...]

    def stage(part):
        cols = slice(part * DN_WIDTH, (part + 1) * DN_WIDTH)
        xs_ref[halo:halo + tm, cols] = _dot(n, w_ref[:, cols])

    def dn_front(part):
        cols = slice(part * DN_WIDTH, (part + 1) * DN_WIDTH)
        for c in range(tm // piece):
            x = xs_ref[c * piece:c * piece + piece + halo, cols]
            acc = cw[DN_CONV - 1:DN_CONV, cols] * x[halo:halo + piece]
            for j in range(DN_CONV - 1):
                lo = halo - (DN_CONV - 1) + j
                acc = acc + cw[j:j + 1, cols] * x[lo:lo + piece]
            a = _silu(acc)
            rows = slice(c * piece, (c + 1) * piece)
            if part == 2:
                qkv_ref[rows, cols] = a
                continue
            for h in range(DN_HEADS):
                y = a[:, h * DN_DIM:(h + 1) * DN_DIM]
                inv = lax.rsqrt(jnp.sum(y * y, axis=-1, keepdims=True) + NORM_EPS)
                if part == 0:
                    inv = inv * (DN_DIM ** -0.5)
                qkv_ref[rows, part * DN_WIDTH + h * DN_DIM:part * DN_WIDTH + (h + 1) * DN_DIM] = y * inv

    stage(0)
    stage(1)
    dn_front(0)
    stage(2)
    dn_front(1)
    z_ref[...] = _silu(_dot(n, w_ref[:, o_z:o_swa]))
    swa_ref[...] = _dot(n, w_ref[:, o_swa:o_sg])
    dn_front(2)
    ba_ref[...] = _dot(n, w_ref[:, o_ba:o_ba + LANES])

    uv = jax.nn.gelu(_dot(n, w_ref[:, o_sg:o_ba]))
    u = uv[:, :SG_WIDTH]
    v = uv[:, SG_WIDTH:]
    mu = jnp.mean(v, axis=-1, keepdims=True)
    vc = v - mu
    var = jnp.mean(vc * vc, axis=-1, keepdims=True)
    vn = vc * lax.rsqrt(var + NORM_EPS) * lg_ref[...] + lb_ref[...]
    row = lax.broadcasted_iota(jnp.int32, (SG_CHUNK, SG_CHUNK), 0)
    col = lax.broadcasted_iota(jnp.int32, (SG_CHUNK, SG_CHUNK), 1)
    causal = row >= col
    low_lanes = col < SG_DIM
    w_causal = [jnp.where(causal, wsp_ref[g], 0.0).astype(BF16) for g in range(SG_GROUPS)]
    for c in range(tm // SG_CHUNK):
        rows = slice(c * SG_CHUNK, (c + 1) * SG_CHUNK)
        for p in range(SG_GROUPS // 2):
            lanes = slice(p * LANES, (p + 1) * LANES)
            vp = vn[rows, lanes]
            v_lo = jnp.where(low_lanes, vp, 0.0).astype(BF16)
            v_hi = jnp.where(low_lanes, 0.0, vp).astype(BF16)
            mixed = _dot(w_causal[2 * p], v_lo) + _dot(w_causal[2 * p + 1], v_hi) + bsp_ref[:, lanes]
            c_ref[rows, lanes] = (u[rows, lanes] * mixed).astype(BF16)

    xs_ref[0:halo, :] = xs_ref[tm:tm + halo, :]


def _proj(h, gain, w_all, layer, conv_w, sg_gain, sg_bias, w_sp, b_sp_rows, seq, tm=512):
    t, d = h.shape

    def row(w):
        return pl.BlockSpec((tm, w), lambda i: (i, 0))

    return pl.pallas_call(
        functools.partial(_proj_kernel, tm=tm, tiles_per_seq=seq // tm),
        grid=(t // tm,),
        in_specs=[row(d), pl.BlockSpec((1, d), lambda i: (0, 0)), _layer_spec(w_all, layer),
                  pl.BlockSpec((DN_CONV, 3 * DN_WIDTH), lambda i: (0, 0)),
                  pl.BlockSpec((1, SG_WIDTH), lambda i: (0, 0)),
                  pl.BlockSpec((1, SG_WIDTH), lambda i: (0, 0)),
                  _resident((SG_GROUPS, SG_CHUNK, SG_CHUNK)),
                  _resident((SG_CHUNK, SG_WIDTH))],
        out_specs=[row(3 * DN_WIDTH), row(DN_WIDTH), row(LANES), row(3 * SWA_WIDTH), row(SG_WIDTH)],
        out_shape=[jax.ShapeDtypeStruct((t, 3 * DN_WIDTH), F32),
                   jax.ShapeDtypeStruct((t, DN_WIDTH), F32),
                   jax.ShapeDtypeStruct((t, LANES), F32),
                   jax.ShapeDtypeStruct((t, 3 * SWA_WIDTH), F32),
                   jax.ShapeDtypeStruct((t, SG_WIDTH), BF16)],
        scratch_shapes=[pltpu.VMEM((tm + 8, 3 * DN_WIDTH), F32)],
        compiler_params=_params("arbitrary"),
        name="mix_proj",
    )(h, gain.reshape(1, d), w_all, conv_w, sg_gain.reshape(1, -1), sg_bias.reshape(1, -1), w_sp, b_sp_rows)


def _split3_bf16(a):
    hi = a.astype(BF16)
    r1 = a - hi.astype(F32)
    mid = r1.astype(BF16)
    lo = (r1 - mid.astype(F32)).astype(BF16)
    return hi, mid, lo


def _dot_exact_lhs(lhs16, rhs):
    hi, mid, lo = _split3_bf16(rhs)
    return _dot(lhs16, hi) + _dot(lhs16, mid) + _dot(lhs16, lo)


def _dn_kernel(qkv_ref, z_ref, ba_ref, gp_ref, og_ref, o_ref, st_ref, *, ts, batch):
    c_len = DN_CHUNK
    heads = range(DN_HEADS)
    seqs = range(batch)

    @pl.when(pl.program_id(0) == 0)
    def _():
        st_ref[...] = jnp.zeros_like(st_ref)

    neg_a = -jnp.exp(gp_ref[0:1, :])
    dt_bias = gp_ref[1:2, :]
    out_gain = og_ref[...]

    row = lax.broadcasted_iota(jnp.int32, (c_len, c_len), 0)
    col = lax.broadcasted_iota(jnp.int32, (c_len, c_len), 1)
    tri_incl = row >= col
    tri_strict = row > col
    eye = (row == col).astype(F32)
    tri16 = tri_incl.astype(BF16)
    ones16 = jnp.ones((c_len, c_len), BF16)
    base_mask = ((row >> 1) == (col >> 1)).astype(F32)
    level_masks = []
    sh = 1
    while (1 << sh) < c_len:
        level_masks.append((((row >> (sh + 1)) == (col >> (sh + 1))) & ((row >> sh) != (col >> sh))).astype(F32))
        sh += 1
    rowh = lax.broadcasted_iota(jnp.int32, (c_len, DN_HEADS * LANES), 0)
    colh = lax.broadcasted_iota(jnp.int32, (c_len, DN_HEADS * LANES), 1) & (LANES - 1)
    upto = (rowh <= colh).astype(F32)

    def prepare(j):
        rows = slice(j * c_len, (j + 1) * c_len)
        units = [(b, h) for b in seqs for h in heads]
        beta_all, gcol, grow = [], [], []
        for b in seqs:
            ba = ba_ref[b, rows, :]
            beta_all.append(jax.nn.sigmoid(ba))
            g_all = neg_a * jax.nn.softplus(ba + dt_bias)
            gseg = jnp.concatenate(
                [jnp.broadcast_to(g_all[:, DN_HEADS + h:DN_HEADS + h + 1], (c_len, LANES)) for h in heads], axis=1)
            gcol.append(_dot_exact_lhs(tri16, gseg))
            grow.append(_dot_exact_lhs(ones16, gseg * upto))

        def head_lanes(b, base, h):
            return qkv_ref[b, rows, base + h * DN_DIM:base + (h + 1) * DN_DIM]

        q = [head_lanes(b, 0, h) for b, h in units]
        k = [head_lanes(b, DN_WIDTH, h) for b, h in units]
        v = [head_lanes(b, 2 * DN_WIDTH, h) for b, h in units]
        beta = [beta_all[b][:, h:h + 1] for b, h in units]
        gcb = [gcol[b][:, h * LANES:(h + 1) * LANES] for b, h in units]
        grow_h = [grow[b][:, h * LANES:(h + 1) * LANES] for b, h in units]
        g_last = [t[:, c_len:c_len + 1] for t in grow_h]
        decay = [jnp.where(tri_incl, jnp.exp(jnp.where(tri_incl, gc[:, :c_len] - gr[:, :c_len], 0.0)), 0.0)
                 for gc, gr in zip(gcb, grow_h)]
        kb = [t * bb for t, bb in zip(k, beta)]
        k16 = [t.astype(BF16) for t in k]
        both = [_dot_nt(jnp.concatenate([t, qq], axis=0).astype(BF16), t16) for t, qq, t16 in zip(kb, q, k16)]
        lower = [jnp.where(tri_strict, t[:c_len] * d, 0.0) for t, d in zip(both, decay)]
        a_qk = [jnp.where(tri_incl, t[c_len:] * d, 0.0).astype(BF16) for t, d in zip(both, decay)]
        t_inv = [eye - lo * base_mask for lo in lower]
        for m in level_masks:
            t16 = [t.astype(BF16) for t in t_inv]
            left = [_dot(t, (lo * m).astype(BF16)).astype(BF16) for t, lo in zip(t16, lower)]
            t_inv = [t - _dot(l, t_b) for t, l, t_b in zip(t_inv, left, t16)]
        t16 = [t.astype(BF16) for t in t_inv]
        egc = [jnp.exp(t) for t in gcb]
        uw = [_dot(t, jnp.concatenate([vv * bb, kk * e], axis=1).astype(BF16))
              for t, vv, bb, kk, e in zip(t16, v, beta, kb, egc)]
        u = [t[:, :DN_DIM] for t in uw]
        k_tail = [(kk * jnp.exp(gl - gc)).astype(BF16) for kk, gl, gc in zip(k, g_last, gcb)]
        wq = [jnp.concatenate([t[:, DN_DIM:], qq * e], axis=0).astype(BF16) for t, qq, e in zip(uw, q, egc)]
        s_decay = [jnp.exp(gl[0:1, :]) for gl in g_last]
        return u, wq, a_qk, k_tail, s_decay

    def advance(j, prepared, state):
        rows = slice(j * c_len, (j + 1) * c_len)
        units = [(b, h) for b in seqs for h in heads]
        u, wq, a_qk, k_tail, s_decay = prepared
        ws = [_dot(t, s.astype(BF16)) for t, s in zip(wq, state)]
        vn16 = [(uu - t[:c_len]).astype(BF16) for uu, t in zip(u, ws)]
        new_state = [s * d + _dot_tn(kt, vn) for s, d, kt, vn in zip(state, s_decay, k_tail, vn16)]
        o = [t[c_len:] + _dot(aq, vn) for t, aq, vn in zip(ws, a_qk, vn16)]
        o = [t * lax.rsqrt(jnp.mean(t * t, axis=-1, keepdims=True) + NORM_EPS) * out_gain for t in o]
        for (b, h), t in zip(units, o):
            lanes = slice(h * DN_DIM, (h + 1) * DN_DIM)
            o_ref[b, rows, lanes] = (t * z_ref[b, rows, lanes]).astype(o_ref.dtype)
        return new_state

    n_units = len(seqs) * len(heads)
    state = [st_ref[i] for i in range(n_units)]
    pending = prepare(0)
    for j in range(1, ts // c_len):
        nxt = prepare(j)
        state = advance(j - 1, pending, state)
        pending = nxt
    state = advance(ts // c_len - 1, pending, state)
    for i in range(n_units):
        st_ref[i] = state[i]


def _deltanet(qkv, z_gate, ba, gate_params, out_gain, batch, ts=256):
    t = qkv.shape[0]
    seq = t // batch

    def rows(w):
        return pl.BlockSpec((batch, ts, w), lambda s: (0, s, 0))

    out = pl.pallas_call(
        functools.partial(_dn_kernel, ts=ts, batch=batch),
        grid=(seq // ts,),
        in_specs=[rows(3 * DN_WIDTH), rows(DN_WIDTH), rows(LANES),
                  pl.BlockSpec((8, LANES), lambda s: (0, 0)),
                  pl.BlockSpec((1, DN_DIM), lambda s: (0, 0))],
        out_specs=rows(DN_WIDTH),
        out_shape=jax.ShapeDtypeStruct((batch, seq, DN_WIDTH), BF16),
        scratch_shapes=[pltpu.VMEM((batch * DN_HEADS, DN_DIM, DN_DIM), F32)],
        compiler_params=_params("arbitrary"),
        name="deltanet",
    )(qkv.reshape(batch, seq, -1), z_gate.reshape(batch, seq, -1), ba.reshape(batch, seq, -1), gate_params,
      out_gain.reshape(1, DN_DIM))
    return out.reshape(t, DN_WIDTH)


def _swa_bias(slopes, dilation):
    blk = SWA_BLOCK
    shape = (2 * blk, 2 * blk)
    row = lax.broadcasted_iota(jnp.int32, shape, 0)
    col = lax.broadcasted_iota(jnp.int32, shape, 1)
    rel = (row & (blk - 1)) + blk - col
    slope = jnp.where(row < blk, slopes[0] * dilation, slopes[1] * dilation)
    return jnp.where((rel >= 0) & (rel <= blk), -slope * rel.astype(F32), NEG_BIG)


def _swa_kernel(q0_ref, q1_ref, k0_ref, k1_ref, v0_ref, v1_ref, o_ref, kv_scr, acc_scr, lse_scr, *, tq, group):
    blk = SWA_BLOCK
    q_refs = (q0_ref, q1_ref)
    pairs = range(SWA_HEADS // 2)
    n_kv = 2 * len(pairs)
    have_prev = pl.program_id(1) > 0

    @pl.when(jnp.logical_not(have_prev))
    def _():
        kv_scr[:, 0:tq, :] = jnp.zeros((n_kv, tq, LANES), F32)

    for c, ref in enumerate((k0_ref, k1_ref, v0_ref, v1_ref)):
        kv_scr[c, tq:2 * tq, :] = ref[...]

    low = lax.broadcasted_iota(jnp.int32, (blk, LANES), 1) < SWA_DIM
    same_block = lax.broadcasted_iota(jnp.int32, (2 * blk, 2 * blk), 1) >= blk
    slopes = [2.0 ** (-8.0 * (h + 1) / SWA_HEADS) for h in range(SWA_HEADS)]

    def run_group(pi, dilation, bias, starts, prev_valid):
        units = range(len(starts))
        if dilation > 1:
            q_rows = [pl.ds(s, blk, stride=dilation) for s in starts]
            kv_rows = [pl.ds(tq + s - dilation * blk, 2 * blk, stride=dilation) for s in starts]
        else:
            q_rows = [pl.ds(s, blk) for s in starts]
            kv_rows = [pl.ds(tq + s - blk, 2 * blk) for s in starts]
        up = [(u, p) for u in units for p in pairs]
        q = [(q_refs[p][q_rows[u], :] * (SWA_DIM ** -0.5)).astype(BF16) for u, p in up]
        zero = jnp.zeros_like(q[0])
        q2 = [jnp.concatenate([jnp.where(low, t, zero), jnp.where(low, zero, t)], axis=0) for t in q]
        k2 = [kv_scr[p, kv_rows[u], :].astype(BF16) for u, p in up]
        v2 = [kv_scr[len(pairs) + p, kv_rows[u], :].astype(BF16) for u, p in up]

        def unit_bias(u, p):
            if prev_valid[u] is None:
                return bias[p]
            return jnp.where(jnp.logical_or(same_block, prev_valid[u]), bias[p], NEG_BIG)

        s = [_dot_nt(a, b) + unit_bias(u, p) for (u, p), a, b in zip(up, q2, k2)]
        m = [jnp.max(t, axis=-1, keepdims=True) for t in s]
        e = [jnp.exp(t - mm) for t, mm in zip(s, m)]
        den = [jnp.sum(t, axis=-1, keepdims=True) for t in e]
        pv = [_dot(t.astype(BF16), b) for t, b in zip(e, v2)]
        for (u, p), mm, dd, o in zip(up, m, den, pv):
            inv = 1.0 / dd
            lse = mm + jnp.log(dd)
            acc_scr[pi, p, q_rows[u], :] = jnp.where(low, o[:blk] * inv[:blk], o[blk:] * inv[blk:])
            lse_scr[pi, p, q_rows[u], :] = jnp.where(low, lse[:blk], lse[blk:])

    for pi, (_, dilation) in enumerate(SWA_PATTERNS):
        bias = [_swa_bias(slopes[2 * p:2 * p + 2], dilation) for p in pairs]
        per_residue = tq // (dilation * blk)
        if per_residue >= group:
            n_trips = per_residue // group
            for r in range(dilation):
                def trip(g, carry, r=r, pi=pi, dilation=dilation, bias=bias):
                    starts = [r + dilation * blk * (g * group + u) for u in range(group)]
                    first = jnp.logical_or(have_prev, g > 0)
                    run_group(pi, dilation, bias, starts, [first] + [None] * (group - 1))
                    return carry
                if n_trips == 1:
                    trip(0, 0)
                else:
                    lax.fori_loop(0, n_trips, trip, 0)
        else:
            assert per_residue == 1 and dilation % group == 0
            for r0 in range(0, dilation, group):
                run_group(pi, dilation, bias, [r0 + u for u in range(group)], [have_prev] * group)

    step = 256
    n_pat = len(SWA_PATTERNS)
    for c in range(tq // step):
        rows = slice(c * step, (c + 1) * step)
        for p in pairs:
            lses = [lse_scr[pi, p, rows, :] for pi in range(n_pat)]
            top = functools.reduce(jnp.maximum, lses)
            es = [jnp.exp(l - top) for l in lses]
            inv = 1.0 / functools.reduce(lambda a, b: a + b, es)
            merged = functools.reduce(lambda a, b: a + b, [es[pi] * acc_scr[pi, p, rows, :] for pi in range(n_pat)])
            o_ref[rows, p * LANES:(p + 1) * LANES] = (merged * inv).astype(o_ref.dtype)

    kv_scr[:, 0:tq, :] = kv_scr[:, tq:2 * tq, :]


def _swa(swa_qkv, batch, tq=2048, group=4):
    t = swa_qkv.shape[0]
    nj = t // batch // tq
    n_tiles = SWA_WIDTH // LANES
    n_pat = len(SWA_PATTERNS)
    return pl.pallas_call(
        functools.partial(_swa_kernel, tq=tq, group=group),
        grid=(batch, nj),
        in_specs=[pl.BlockSpec((tq, LANES), functools.partial(lambda b, j, c: (b * nj + j, c), c=c))
                  for c in range(3 * n_tiles)],
        out_specs=pl.BlockSpec((tq, SWA_WIDTH), lambda b, j: (b * nj + j, 0)),
        out_shape=jax.ShapeDtypeStruct((t, SWA_WIDTH), BF16),
        scratch_shapes=[pltpu.VMEM((2 * n_tiles, 2 * tq, LANES), F32),
                        pltpu.VMEM((n_pat, n_tiles, tq, LANES), F32),
                        pltpu.VMEM((n_pat, n_tiles, tq, LANES), F32)],
        compiler_params=_params("parallel", "arbitrary"),
        name="swa",
    )(*([swa_qkv] * (3 * n_tiles)))


def _kv_kernel(mem_ref, g_ref, w_ref, k_ref, v_ref):
    n = _rms(mem_ref[...], g_ref[...]).astype(BF16)
    d = k_ref.shape[-1]
    k_ref[...] = _dot(n, w_ref[:, :d]).astype(BF16)
    v_ref[...] = _dot(n, w_ref[:, d:]).astype(BF16)


def _mem_kv(mem, gain, w_kv, layer):
    b, m, d = mem.shape
    blk = pl.BlockSpec((None, m, d), lambda i: (i, 0, 0))
    return pl.pallas_call(
        _kv_kernel,
        grid=(b,),
        in_specs=[blk, pl.BlockSpec((1, d), lambda i: (0, 0)), _layer_spec(w_kv, layer)],
        out_specs=[blk, blk],
        out_shape=[jax.ShapeDtypeStruct((b, m, d), BF16)] * 2,
        compiler_params=_params("parallel"),
        name="xa_kv",
    )(mem, gain.reshape(1, d), w_kv)


def _mix_xa_kernel(h_ref, a_ref, b_ref, c_ref, wm_ref, g_ref, wq_ref, k_ref, v_ref, wo_ref, o_ref, att_scr):
    x = h_ref[...] + _dot(a_ref[...], wm_ref[0:DN_WIDTH, :])
    x = x + _dot(b_ref[...], wm_ref[DN_WIDTH:DN_WIDTH + SWA_WIDTH, :])
    x = x + _dot(c_ref[...], wm_ref[DN_WIDTH + SWA_WIDTH:, :])
    n = _rms(x, g_ref[...]).astype(BF16)
    d = x.shape[-1]
    dh = d // XA_HEADS
    q = _dot(n, wq_ref[...]).astype(BF16)
    lanes = [slice(h * dh, (h + 1) * dh) for h in range(XA_HEADS)]
    s = [_dot_nt(q[:, ln], k_ref[:, ln]) * (dh ** -0.5) for ln in lanes]
    m = [jnp.max(t, axis=-1, keepdims=True) for t in s]
    p = [jnp.exp(t - mm) for t, mm in zip(s, m)]
    p = [t * (1.0 / jnp.sum(t, axis=-1, keepdims=True)) for t in p]
    for t, ln in zip(p, lanes):
        att_scr[:, ln] = _dot(t.astype(BF16), v_ref[:, ln]).astype(BF16)
    o_ref[...] = x + _dot(att_scr[...], wo_ref[...])


def _mix_out_cross_attention(h, out_a, out_b, out_c, w_out, gain, w_q, k, v, w_o, layer, batch, tm=1024):
    t, d = h.shape
    ns = t // batch // tm
    m = k.shape[1]

    def row(w):
        return pl.BlockSpec((tm, w), lambda b, s: (b * ns + s, 0))

    kv = pl.BlockSpec((None, m, d), lambda b, s: (b, 0, 0))
    return pl.pallas_call(
        _mix_xa_kernel,
        grid=(batch, ns),
        in_specs=[row(d), row(DN_WIDTH), row(SWA_WIDTH), row(SG_WIDTH), _layer_spec(w_out, layer),
                  pl.BlockSpec((1, d), lambda b, s: (0, 0)), _layer_spec(w_q, layer), kv, kv, _layer_spec(w_o, layer)],
        out_specs=row(d),
        out_shape=jax.ShapeDtypeStruct((t, d), F32),
        scratch_shapes=[pltpu.VMEM((tm, d), BF16)],
        compiler_params=_params("parallel", "parallel"),
        name="mix_out_cross_attn",
    )(h, out_a, out_b, out_c, w_out, gain.reshape(1, d), w_q, k, v, w_o)


def kernel(x, mem, ffn1_norm, ffn1_w_gate_up, ffn1_w_down, mix_norm, mix_w_in, dn_conv_w, dn_a_log, dn_dt_bias,
           dn_out_norm, sg_norm_gain, sg_norm_bias, sg_w_spatial, sg_b_spatial, mix_w_out, xa_norm, xa_mem_norm,
           xa_w_q, xa_w_kv, xa_w_o, ffn2_norm, ffn2_w_gate_up, ffn2_w_down, final_norm):
    batch, seq, d = x.shape
    depth = ffn1_norm.shape[0]
    ffn1_gu, ffn1_d = _to_bf16(ffn1_w_gate_up), _to_bf16(ffn1_w_down)
    ffn2_gu, ffn2_d = _to_bf16(ffn2_w_gate_up), _to_bf16(ffn2_w_down)
    w_in, w_out = _mix_in_weights(mix_w_in), _to_bf16(mix_w_out)
    w_q, w_kv, w_o = _to_bf16(xa_w_q), _to_bf16(xa_w_kv), _to_bf16(xa_w_o)
    h = x.reshape(batch * seq, d)
    for i in range(depth):
        h = _ffn(h, ffn1_norm[i], ffn1_gu, ffn1_d, i)

        gate_params = jnp.zeros((8, LANES), F32)
        gate_params = gate_params.at[0, DN_HEADS:2 * DN_HEADS].set(dn_a_log[i])
        gate_params = gate_params.at[1, DN_HEADS:2 * DN_HEADS].set(dn_dt_bias[i])
        b_sp_rows = jnp.repeat(sg_b_spatial[i].T, SG_DIM, axis=1)
        qkv, z_gate, ba, swa_qkv, out_c = _proj(h, mix_norm[i], w_in, i, dn_conv_w[i], sg_norm_gain[i],
                                                sg_norm_bias[i], sg_w_spatial[i], b_sp_rows, seq)
        out_a = _deltanet(qkv, z_gate, ba, gate_params, dn_out_norm[i], batch)
        out_b = _swa(swa_qkv, batch)
        k, v = _mem_kv(mem, xa_mem_norm[i], w_kv, i)
        h = _mix_out_cross_attention(h, out_a, out_b, out_c, w_out, xa_norm[i], w_q, k, v, w_o, i, batch)
        h = _ffn(h, ffn2_norm[i], ffn2_gu, ffn2_d, i, final_gain=final_norm if i == depth - 1 else None)
    return h.reshape(batch, seq, d)
```

```python
import functools

import jax
import jax.numpy as jnp
from jax import lax
from jax.experimental import pallas as pl
from jax.experimental.pallas import tpu as pltpu

F32 = jnp.float32
BF16 = jnp.bfloat16
NORM_EPS = 1e-6
LANES = 128
SUBLANES = 8
VMEM_LIMIT = 56 * 1024 * 1024

DN_HEADS = 4
DN_DIM = 128
DN_WIDTH = DN_HEADS * DN_DIM
DN_CONV = 4
DN_CHUNK = 64
SWA_HEADS = 4
SWA_DIM = 64
SWA_WIDTH = SWA_HEADS * SWA_DIM
SWA_PATTERNS = ((128, 1), (512, 4), (2048, 16))
SWA_BLOCK = 128
SG_GROUPS = 4
SG_DIM = 64
SG_WIDTH = SG_GROUPS * SG_DIM
SG_CHUNK = 128
XA_HEADS = 4
NEG_BIG = -1e30


def _resident(shape):
    return pl.BlockSpec(shape, lambda *_: (0,) * len(shape), pipeline_mode=pl.Buffered(1))


def _layer_spec(stacked, layer):
    return pl.BlockSpec((None,) + stacked.shape[1:], lambda *_: (layer, 0, 0), pipeline_mode=pl.Buffered(1))


def _params(*sem):
    return pltpu.CompilerParams(dimension_semantics=sem, vmem_limit_bytes=VMEM_LIMIT)


def _rms(x, gain):
    ms = jnp.mean(x * x, axis=-1, keepdims=True)
    return x * lax.rsqrt(ms + NORM_EPS) * gain


def _dot(a, b):
    return jnp.dot(a, b, preferred_element_type=F32)


def _dot_nt(a, b):
    return lax.dot_general(a, b, (((1,), (1,)), ((), ())), preferred_element_type=F32)


def _dot_tn(a, b):
    return lax.dot_general(a, b, (((0,), (0,)), ((), ())), preferred_element_type=F32)


def _silu(x):
    return x * jax.nn.sigmoid(x)


def _cast_kernel(x_ref, o_ref):
    o_ref[...] = x_ref[...].astype(o_ref.dtype)


def _to_bf16(w, tr=256):
    depth, rows, cols = w.shape
    blk = pl.BlockSpec((None, tr, cols), lambda l, r: (l, r, 0))
    return pl.pallas_call(
        _cast_kernel,
        grid=(depth, rows // tr),
        in_specs=[blk],
        out_specs=blk,
        out_shape=jax.ShapeDtypeStruct(w.shape, BF16),
        compiler_params=_params("parallel", "parallel"),
        name="cast_bf16",
    )(w)


MIX_COLS = 3 * DN_WIDTH + DN_WIDTH + 3 * SWA_WIDTH + 2 * SG_WIDTH + LANES


def _cast_mix_in_kernel(x_ref, o_ref):
    o_b = 4 * DN_WIDTH
    o_swa = o_b + 2 * DN_HEADS
    x = x_ref[...]
    rest = x.shape[1] - o_swa
    o_ref[:, 0:o_b] = x[:, 0:o_b].astype(BF16)
    o_ref[:, o_b:o_b + rest] = x[:, o_swa:].astype(BF16)
    gates = jnp.concatenate([x[:, o_b:o_swa], jnp.zeros((x.shape[0], LANES - 2 * DN_HEADS), F32)], axis=1)
    o_ref[:, o_b + rest:] = gates.astype(BF16)


def _mix_in_weights(w_in, tr=256):
    depth, rows, cols = w_in.shape
    return pl.pallas_call(
        _cast_mix_in_kernel,
        grid=(depth, rows // tr),
        in_specs=[pl.BlockSpec((None, tr, cols), lambda l, r: (l, r, 0))],
        out_specs=pl.BlockSpec((None, tr, MIX_COLS), lambda l, r: (l, r, 0)),
        out_shape=jax.ShapeDtypeStruct((depth, rows, MIX_COLS), BF16),
        compiler_params=_params("parallel", "parallel"),
        name="cast_mix_in",
    )(w_in)


def _ffn_kernel(*refs, d_ff, chunk, final):
    if final:
        x_ref, g_ref, wgu_ref, wd_ref, fg_ref, o_ref, h_scr = refs
    else:
        x_ref, g_ref, wgu_ref, wd_ref, o_ref, h_scr = refs
    x = x_ref[...]
    n = _rms(x, g_ref[...]).astype(BF16)
    for c in range(d_ff // chunk):
        gate = _dot(n, wgu_ref[:, c * chunk:(c + 1) * chunk])
        up = _dot(n, wgu_ref[:, d_ff + c * chunk:d_ff + (c + 1) * chunk])
        h_scr[:, c * chunk:(c + 1) * chunk] = (_silu(gate) * up).astype(BF16)
    y = x + 0.5 * _dot(h_scr[...], wd_ref[...])
    if final:
        y = _rms(y, fg_ref[...])
    o_ref[...] = y


def _ffn(h, gain, w_gu, w_d, layer, final_gain=None, tm=1024, chunk=256):
    t, d = h.shape
    d_ff = w_d.shape[1]
    final = final_gain is not None
    row = pl.BlockSpec((tm, d), lambda i: (i, 0))
    vec = pl.BlockSpec((1, d), lambda i: (0, 0))
    in_specs = [row, vec, _layer_spec(w_gu, layer), _layer_spec(w_d, layer)]
    args = [h, gain.reshape(1, d), w_gu, w_d]
    if final:
        in_specs.append(vec)
        args.append(final_gain.reshape(1, d))
    return pl.pallas_call(
        functools.partial(_ffn_kernel, d_ff=d_ff, chunk=chunk, final=final),
        grid=(t // tm,),
        in_specs=in_specs,
        out_specs=row,
        out_shape=jax.ShapeDtypeStruct((t, d), F32),
        scratch_shapes=[pltpu.VMEM((tm, d_ff), BF16)],
        compiler_params=_params("parallel"),
        name="ffn_final" if final else "ffn",
    )(*args)


def _proj_kernel(x_ref, g_ref, w_ref, cw_ref, lg_ref, lb_ref, wsp_ref, bsp_ref,
                 qkv_ref, z_ref, ba_ref, swa_ref, c_ref, xs_ref, *, tm, tiles_per_seq):
    halo = SUBLANES
    piece = DN_CHUNK
    n = _rms(x_ref[...], g_ref[...]).astype(BF16)
    o_z = 3 * DN_WIDTH
    o_swa = o_z + DN_WIDTH
    o_sg = o_swa + 3 * SWA_WIDTH
    o_ba = o_sg + 2 * SG_WIDTH

    @pl.when(pl.program_id(0) % tiles_per_seq == 0)
    def _():
        xs_ref[0:halo, :] = jnp.zeros((halo, o_z), F32)

    cw = cw_ref[...]

    def stage(part):
        cols = slice(part * DN_WIDTH, (part + 1) * DN_WIDTH)
        xs_ref[halo:halo + tm, cols] = _dot(n, w_ref[:, cols])

    def dn_front(part):
        cols = slice(part * DN_WIDTH, (part + 1) * DN_WIDTH)
        for c in range(tm // piece):
            x = xs_ref[c * piece:c * piece + piece + halo, cols]
            acc = cw[DN_CONV - 1:DN_CONV, cols] * x[halo:halo + piece]
            for j in range(DN_CONV - 1):
                lo = halo - (DN_CONV - 1) + j
                acc = acc + cw[j:j + 1, cols] * x[lo:lo + piece]
            a = _silu(acc)
            rows = slice(c * piece, (c + 1) * piece)
            if part == 2:
                qkv_ref[rows, cols] = a
                continue
            for h in range(DN_HEADS):
                y = a[:, h * DN_DIM:(h + 1) * DN_DIM]
                inv = lax.rsqrt(jnp.sum(y * y, axis=-1, keepdims=True) + NORM_EPS)
                if part == 0:
                    inv = inv * (DN_DIM ** -0.5)
                qkv_ref[rows, part * DN_WIDTH + h * DN_DIM:part * DN_WIDTH + (h + 1) * DN_DIM] = y * inv

    stage(0)
    stage(1)
    dn_front(0)
    stage(2)
    dn_front(1)
    z_ref[...] = _silu(_dot(n, w_ref[:, o_z:o_swa]))
    swa_ref[...] = _dot(n, w_ref[:, o_swa:o_sg])
    dn_front(2)
    ba_ref[...] = _dot(n, w_ref[:, o_ba:o_ba + LANES])

    uv = jax.nn.gelu(_dot(n, w_ref[:, o_sg:o_ba]))
    u = uv[:, :SG_WIDTH]
    v = uv[:, SG_WIDTH:]
    mu = jnp.mean(v, axis=-1, keepdims=True)
    vc = v - mu
    var = jnp.mean(vc * vc, axis=-1, keepdims=True)
    vn = vc * lax.rsqrt(var + NORM_EPS) * lg_ref[...] + lb_ref[...]
    row = lax.broadcasted_iota(jnp.int32, (SG_CHUNK, SG_CHUNK), 0)
    col = lax.broadcasted_iota(jnp.int32, (SG_CHUNK, SG_CHUNK), 1)
    causal = row >= col
    low_lanes = col < SG_DIM
    w_causal = [jnp.where(causal, wsp_ref[g], 0.0).astype(BF16) for g in range(SG_GROUPS)]
    for c in range(tm // SG_CHUNK):
        rows = slice(c * SG_CHUNK, (c + 1) * SG_CHUNK)
        for p in range(SG_GROUPS // 2):
            lanes = slice(p * LANES, (p + 1) * LANES)
            vp = vn[rows, lanes]
            v_lo = jnp.where(low_lanes, vp, 0.0).astype(BF16)
            v_hi = jnp.where(low_lanes, 0.0, vp).astype(BF16)
            mixed = _dot(w_causal[2 * p], v_lo) + _dot(w_causal[2 * p + 1], v_hi) + bsp_ref[:, lanes]
            c_ref[rows, lanes] = (u[rows, lanes] * mixed).astype(BF16)

    xs_ref[0:halo, :] = xs_ref[tm:tm + halo, :]


def _proj(h, gain, w_all, layer, conv_w, sg_gain, sg_bias, w_sp, b_sp_rows, seq, tm=512):
    t, d = h.shape

    def row(w):
        return pl.BlockSpec((tm, w), lambda i: (i, 0))

    return pl.pallas_call(
        functools.partial(_proj_kernel, tm=tm, tiles_per_seq=seq // tm),
        grid=(t // tm,),
        in_specs=[row(d), pl.BlockSpec((1, d), lambda i: (0, 0)), _layer_spec(w_all, layer),
                  pl.BlockSpec((DN_CONV, 3 * DN_WIDTH), lambda i: (0, 0)),
                  pl.BlockSpec((1, SG_WIDTH), lambda i: (0, 0)),
                  pl.BlockSpec((1, SG_WIDTH), lambda i: (0, 0)),
                  _resident((SG_GROUPS, SG_CHUNK, SG_CHUNK)),
                  _resident((SG_CHUNK, SG_WIDTH))],
        out_specs=[row(3 * DN_WIDTH), row(DN_WIDTH), row(LANES), row(3 * SWA_WIDTH), row(SG_WIDTH)],
        out_shape=[jax.ShapeDtypeStruct((t, 3 * DN_WIDTH), F32),
                   jax.ShapeDtypeStruct((t, DN_WIDTH), F32),
                   jax.ShapeDtypeStruct((t, LANES), F32),
                   jax.ShapeDtypeStruct((t, 3 * SWA_WIDTH), F32),
                   jax.ShapeDtypeStruct((t, SG_WIDTH), BF16)],
        scratch_shapes=[pltpu.VMEM((tm + 8, 3 * DN_WIDTH), F32)],
        compiler_params=_params("arbitrary"),
        name="mix_proj",
    )(h, gain.reshape(1, d), w_all, conv_w, sg_gain.reshape(1, -1), sg_bias.reshape(1, -1), w_sp, b_sp_rows)


def _split3_bf16(a):
    hi = a.astype(BF16)
    r1 = a - hi.astype(F32)
    mid = r1.astype(BF16)
    lo = (r1 - mid.astype(F32)).astype(BF16)
    return hi, mid, lo


def _dot_exact_lhs(lhs16, rhs):
    hi, mid, lo = _split3_bf16(rhs)
    return _dot(lhs16, hi) + _dot(lhs16, mid) + _dot(lhs16, lo)


def _dn_kernel(qkv_ref, z_ref, ba_ref, gp_ref, og_ref, o_ref, st_ref, *, ts, batch):
    c_len = DN_CHUNK
    heads = range(DN_HEADS)
    seqs = range(batch)

    @pl.when(pl.program_id(0) == 0)
    def _():
        st_ref[...] = jnp.zeros_like(st_ref)

    neg_a = -jnp.exp(gp_ref[0:1, :])
    dt_bias = gp_ref[1:2, :]
    out_gain = og_ref[...]

    row = lax.broadcasted_iota(jnp.int32, (c_len, c_len), 0)
    col = lax.broadcasted_iota(jnp.int32, (c_len, c_len), 1)
    tri_incl = row >= col
    tri_strict = row > col
    eye = (row == col).astype(F32)
    tri16 = tri_incl.astype(BF16)
    ones16 = jnp.ones((c_len, c_len), BF16)
    base_mask = ((row >> 1) == (col >> 1)).astype(F32)
    level_masks = []
    sh = 1
    while (1 << sh) < c_len:
        level_masks.append((((row >> (sh + 1)) == (col >> (sh + 1))) & ((row >> sh) != (col >> sh))).astype(F32))
        sh += 1
    rowh = lax.broadcasted_iota(jnp.int32, (c_len, DN_HEADS * LANES), 0)
    colh = lax.broadcasted_iota(jnp.int32, (c_len, DN_HEADS * LANES), 1) & (LANES - 1)
    upto = (rowh <= colh).astype(F32)

    def prepare(j):
        rows = slice(j * c_len, (j + 1) * c_len)
        units = [(b, h) for b in seqs for h in heads]
        beta_all, gcol, grow = [], [], []
        for b in seqs:
            ba = ba_ref[b, rows, :]
            beta_all.append(jax.nn.sigmoid(ba))
            g_all = neg_a * jax.nn.softplus(ba + dt_bias)
            gseg = jnp.concatenate(
                [jnp.broadcast_to(g_all[:, DN_HEADS + h:DN_HEADS + h + 1], (c_len, LANES)) for h in heads], axis=1)
            gcol.append(_dot_exact_lhs(tri16, gseg))
            grow.append(_dot_exact_lhs(ones16, gseg * upto))

        def head_lanes(b, base, h):
            return qkv_ref[b, rows, base + h * DN_DIM:base + (h + 1) * DN_DIM]

        q = [head_lanes(b, 0, h) for b, h in units]
        k = [head_lanes(b, DN_WIDTH, h) for b, h in units]
        v = [head_lanes(b, 2 * DN_WIDTH, h) for b, h in units]
        beta = [beta_all[b][:, h:h + 1] for b, h in units]
        gcb = [gcol[b][:, h * LANES:(h + 1) * LANES] for b, h in units]
        grow_h = [grow[b][:, h * LANES:(h + 1) * LANES] for b, h in units]
        g_last = [t[:, c_len:c_len + 1] for t in grow_h]
        decay = [jnp.where(tri_incl, jnp.exp(jnp.where(tri_incl, gc[:, :c_len] - gr[:, :c_len], 0.0)), 0.0)
                 for gc, gr in zip(gcb, grow_h)]
        kb = [t * bb for t, bb in zip(k, beta)]
        k16 = [t.astype(BF16) for t in k]
        both = [_dot_nt(jnp.concatenate([t, qq], axis=0).astype(BF16), t16) for t, qq, t16 in zip(kb, q, k16)]
        lower = [jnp.where(tri_strict, t[:c_len] * d, 0.0) for t, d in zip(both, decay)]
        a_qk = [jnp.where(tri_incl, t[c_len:] * d, 0.0).astype(BF16) for t, d in zip(both, decay)]
        t_inv = [eye - lo * base_mask for lo in lower]
        for m in level_masks:
            t16 = [t.astype(BF16) for t in t_inv]
            left = [_dot(t, (lo * m).astype(BF16)).astype(BF16) for t, lo in zip(t16, lower)]
            t_inv = [t - _dot(l, t_b) for t, l, t_b in zip(t_inv, left, t16)]
        t16 = [t.astype(BF16) for t in t_inv]
        egc = [jnp.exp(t) for t in gcb]
        uw = [_dot(t, jnp.concatenate([vv * bb, kk * e], axis=1).astype(BF16))
              for t, vv, bb, kk, e in zip(t16, v, beta, kb, egc)]
        u = [t[:, :DN_DIM] for t in uw]
        k_tail = [(kk * jnp.exp(gl - gc)).astype(BF16) for kk, gl, gc in zip(k, g_last, gcb)]
        wq = [jnp.concatenate([t[:, DN_DIM:], qq * e], axis=0).astype(BF16) for t, qq, e in zip(uw, q, egc)]
        s_decay = [jnp.exp(gl[0:1, :]) for gl in g_last]
        return u, wq, a_qk, k_tail, s_decay

    def advance(j, prepared, state):
        rows = slice(j * c_len, (j + 1) * c_len)
        units = [(b, h) for b in seqs for h in heads]
        u, wq, a_qk, k_tail, s_decay = prepared
        ws = [_dot(t, s.astype(BF16)) for t, s in zip(wq, state)]
        vn16 = [(uu - t[:c_len]).astype(BF16) for uu, t in zip(u, ws)]
        new_state = [s * d + _dot_tn(kt, vn) for s, d, kt, vn in zip(state, s_decay, k_tail, vn16)]
        o = [t[c_len:] + _dot(aq, vn) for t, aq, vn in zip(ws, a_qk, vn16)]
        o = [t * lax.rsqrt(jnp.mean(t * t, axis=-1, keepdims=True) + NORM_EPS) * out_gain for t in o]
        for (b, h), t in zip(units, o):
            lanes = slice(h * DN_DIM, (h + 1) * DN_DIM)
            o_ref[b, rows, lanes] = (t * z_ref[b, rows, lanes]).astype(o_ref.dtype)
        return new_state

    n_units = len(seqs) * len(heads)
    state = [st_ref[i] for i in range(n_units)]
    pending = prepare(0)
    for j in range(1, ts // c_len):
        nxt = prepare(j)
        state = advance(j - 1, pending, state)
        pending = nxt
    state = advance(ts // c_len - 1, pending, state)
    for i in range(n_units):
        st_ref[i] = state[i]


def _deltanet(qkv, z_gate, ba, gate_params, out_gain, batch, ts=256):
    t = qkv.shape[0]
    seq = t // batch

    def rows(w):
        return pl.BlockSpec((batch, ts, w), lambda s: (0, s, 0))

    out = pl.pallas_call(
        functools.partial(_dn_kernel, ts=ts, batch=batch),
        grid=(seq // ts,),
        in_specs=[rows(3 * DN_WIDTH), rows(DN_WIDTH), rows(LANES),
                  pl.BlockSpec((8, LANES), lambda s: (0, 0)),
                  pl.BlockSpec((1, DN_DIM), lambda s: (0, 0))],
        out_specs=rows(DN_WIDTH),
        out_shape=jax.ShapeDtypeStruct((batch, seq, DN_WIDTH), BF16),
        scratch_shapes=[pltpu.VMEM((batch * DN_HEADS, DN_DIM, DN_DIM), F32)],
        compiler_params=_params("arbitrary"),
        name="deltanet",
    )(qkv.reshape(batch, seq, -1), z_gate.reshape(batch, seq, -1), ba.reshape(batch, seq, -1), gate_params,
      out_gain.reshape(1, DN_DIM))
    return out.reshape(t, DN_WIDTH)


def _swa_bias(slopes, dilation):
    blk = SWA_BLOCK
    shape = (2 * blk, 2 * blk)
    row = lax.broadcasted_iota(jnp.int32, shape, 0)
    col = lax.broadcasted_iota(jnp.int32, shape, 1)
    rel = (row & (blk - 1)) + blk - col
    slope = jnp.where(row < blk, slopes[0] * dilation, slopes[1] * dilation)
    return jnp.where((rel >= 0) & (rel <= blk), -slope * rel.astype(F32), NEG_BIG)


def _swa_kernel(q0_ref, q1_ref, k0_ref, k1_ref, v0_ref, v1_ref, pk0_ref, pk1_ref, pv0_ref, pv1_ref,
                o_ref, acc_scr, lse_scr, *, tq, group):
    blk = SWA_BLOCK
    q_refs = (q0_ref, q1_ref)
    kv_refs = ((k0_ref, k1_ref), (v0_ref, v1_ref))
    prev_refs = ((pk0_ref, pk1_ref), (pv0_ref, pv1_ref))
    pairs = range(SWA_HEADS // 2)
    have_prev = pl.program_id(1) > 0

    low = lax.broadcasted_iota(jnp.int32, (blk, LANES), 1) < SWA_DIM
    same_block = lax.broadcasted_iota(jnp.int32, (2 * blk, 2 * blk), 1) >= blk
    slopes = [2.0 ** (-8.0 * (h + 1) / SWA_HEADS) for h in range(SWA_HEADS)]

    def rows_of(start, count, dilation):
        return pl.ds(start, count, stride=dilation) if dilation > 1 else pl.ds(start, count)

    def run_group(pi, dilation, bias, starts, crosses_tile):
        units = range(len(starts))
        q_rows = [rows_of(s, blk, dilation) for s in starts]
        up = [(u, p) for u in units for p in pairs]
        q = [(q_refs[p][q_rows[u], :] * (SWA_DIM ** -0.5)).astype(BF16) for u, p in up]
        zero = jnp.zeros_like(q[0])
        q2 = [jnp.concatenate([jnp.where(low, t, zero), jnp.where(low, zero, t)], axis=0) for t in q]

        def keys_or_values(which, u, p):
            if crosses_tile[u]:
                before = prev_refs[which][p][rows_of(tq + starts[u] - dilation * blk, blk, dilation), :]
                return jnp.concatenate([before, kv_refs[which][p][q_rows[u], :]], axis=0).astype(BF16)
            return kv_refs[which][p][rows_of(starts[u] - dilation * blk, 2 * blk, dilation), :].astype(BF16)

        k2 = [keys_or_values(0, u, p) for u, p in up]
        v2 = [keys_or_values(1, u, p) for u, p in up]

        def unit_bias(u, p):
            if not crosses_tile[u]:
                return bias[p]
            return jnp.where(jnp.logical_or(same_block, have_prev), bias[p], NEG_BIG)

        s = [_dot_nt(a, b) + unit_bias(u, p) for (u, p), a, b in zip(up, q2, k2)]
        m = [jnp.max(t, axis=-1, keepdims=True) for t in s]
        e = [jnp.exp(t - mm) for t, mm in zip(s, m)]
        den = [jnp.sum(t, axis=-1, keepdims=True) for t in e]
        pv = [_dot(t.astype(BF16), b) for t, b in zip(e, v2)]
        for (u, p), mm, dd, o in zip(up, m, den, pv):
            inv = 1.0 / dd
            lse = mm + jnp.log(dd)
            acc_scr[pi, p, q_rows[u], :] = jnp.where(low, o[:blk] * inv[:blk], o[blk:] * inv[blk:])
            lse_scr[pi, p, q_rows[u], :] = jnp.where(low, lse[:blk], lse[blk:])

    for pi, (_, dilation) in enumerate(SWA_PATTERNS):
        bias = [_swa_bias(slopes[2 * p:2 * p + 2], dilation) for p in pairs]
        per_residue = tq // (dilation * blk)
        if per_residue >= group:
            n_trips = per_residue // group
            for r in range(dilation):
                def trip(g, carry, r=r, pi=pi, dilation=dilation, bias=bias):
                    starts = [r + dilation * blk * (g * group + u) for u in range(group)]
                    first = isinstance(g, int) and g == 0
                    run_group(pi, dilation, bias, starts, [first] + [False] * (group - 1))
                    return carry
                trip(0, 0)
                if n_trips > 1:
                    lax.fori_loop(1, n_trips, trip, 0)
        else:
            assert per_residue == 1 and dilation % group == 0
            for r0 in range(0, dilation, group):
                run_group(pi, dilation, bias, [r0 + u for u in range(group)], [True] * group)

    merge_rows = 2 * SWA_BLOCK
    n_pat = len(SWA_PATTERNS)
    for c in range(tq // merge_rows):
        rows = slice(c * merge_rows, (c + 1) * merge_rows)
        for p in pairs:
            lses = [lse_scr[pi, p, rows, :] for pi in range(n_pat)]
            top = functools.reduce(jnp.maximum, lses)
            es = [jnp.exp(l - top) for l in lses]
            inv = 1.0 / functools.reduce(lambda a, b: a + b, es)
            merged = functools.reduce(lambda a, b: a + b, [es[pi] * acc_scr[pi, p, rows, :] for pi in range(n_pat)])
            o_ref[rows, p * LANES:(p + 1) * LANES] = (merged * inv).astype(o_ref.dtype)


def _swa(swa_qkv, batch, tq=2048, group=4):
    t = swa_qkv.shape[0]
    nj = t // batch // tq
    n_tiles = SWA_WIDTH // LANES
    n_pat = len(SWA_PATTERNS)

    def cur(c):
        return pl.BlockSpec((tq, LANES), lambda b, j: (b * nj + j, c))

    def prev(c):
        return pl.BlockSpec((tq, LANES), lambda b, j: (b * nj + jnp.maximum(j - 1, 0), c))

    cols = range(3 * n_tiles)
    kv_cols = range(n_tiles, 3 * n_tiles)
    return pl.pallas_call(
        functools.partial(_swa_kernel, tq=tq, group=group),
        grid=(batch, nj),
        in_specs=[cur(c) for c in cols] + [prev(c) for c in kv_cols],
        out_specs=pl.BlockSpec((tq, SWA_WIDTH), lambda b, j: (b * nj + j, 0)),
        out_shape=jax.ShapeDtypeStruct((t, SWA_WIDTH), BF16),
        scratch_shapes=[pltpu.VMEM((n_pat, n_tiles, tq, LANES), F32),
                        pltpu.VMEM((n_pat, n_tiles, tq, LANES), F32)],
        compiler_params=_params("parallel", "parallel"),
        name="swa",
    )(*([swa_qkv] * (len(cols) + len(kv_cols))))


def _kv_kernel(mem_ref, g_ref, w_ref, k_ref, v_ref):
    n = _rms(mem_ref[...], g_ref[...]).astype(BF16)
    d = k_ref.shape[-1]
    k_ref[...] = _dot(n, w_ref[:, :d]).astype(BF16)
    v_ref[...] = _dot(n, w_ref[:, d:]).astype(BF16)


def _mem_kv(mem, gain, w_kv, layer):
    b, m, d = mem.shape
    blk = pl.BlockSpec((None, m, d), lambda i: (i, 0, 0))
    return pl.pallas_call(
        _kv_kernel,
        grid=(b,),
        in_specs=[blk, pl.BlockSpec((1, d), lambda i: (0, 0)), _layer_spec(w_kv, layer)],
        out_specs=[blk, blk],
        out_shape=[jax.ShapeDtypeStruct((b, m, d), BF16)] * 2,
        compiler_params=_params("parallel"),
        name="xa_kv",
    )(mem, gain.reshape(1, d), w_kv)


def _mix_xa_kernel(h_ref, a_ref, b_ref, c_ref, wm_ref, g_ref, wq_ref, k_ref, v_ref, wo_ref, o_ref, att_scr):
    x = h_ref[...] + _dot(a_ref[...], wm_ref[0:DN_WIDTH, :])
    x = x + _dot(b_ref[...], wm_ref[DN_WIDTH:DN_WIDTH + SWA_WIDTH, :])
    x = x + _dot(c_ref[...], wm_ref[DN_WIDTH + SWA_WIDTH:, :])
    n = _rms(x, g_ref[...]).astype(BF16)
    d = x.shape[-1]
    dh = d // XA_HEADS
    q = _dot(n, wq_ref[...]).astype(BF16)
    lanes = [slice(h * dh, (h + 1) * dh) for h in range(XA_HEADS)]
    s = [_dot_nt(q[:, ln], k_ref[:, ln]) * (dh ** -0.5) for ln in lanes]
    m = [jnp.max(t, axis=-1, keepdims=True) for t in s]
    p = [jnp.exp(t - mm) for t, mm in zip(s, m)]
    p = [t * (1.0 / jnp.sum(t, axis=-1, keepdims=True)) for t in p]
    for t, ln in zip(p, lanes):
        att_scr[:, ln] = _dot(t.astype(BF16), v_ref[:, ln]).astype(BF16)
    o_ref[...] = x + _dot(att_scr[...], wo_ref[...])


def _mix_out_cross_attention(h, out_a, out_b, out_c, w_out, gain, w_q, k, v, w_o, layer, batch, tm=1024):
    t, d = h.shape
    ns = t // batch // tm
    m = k.shape[1]

    def row(w):
        return pl.BlockSpec((tm, w), lambda b, s: (b * ns + s, 0))

    kv = pl.BlockSpec((None, m, d), lambda b, s: (b, 0, 0))
    return pl.pallas_call(
        _mix_xa_kernel,
        grid=(batch, ns),
        in_specs=[row(d), row(DN_WIDTH), row(SWA_WIDTH), row(SG_WIDTH), _layer_spec(w_out, layer),
                  pl.BlockSpec((1, d), lambda b, s: (0, 0)), _layer_spec(w_q, layer), kv, kv, _layer_spec(w_o, layer)],
        out_specs=row(d),
        out_shape=jax.ShapeDtypeStruct((t, d), F32),
        scratch_shapes=[pltpu.VMEM((tm, d), BF16)],
        compiler_params=_params("parallel", "parallel"),
        name="mix_out_cross_attn",
    )(h, out_a, out_b, out_c, w_out, gain.reshape(1, d), w_q, k, v, w_o)


def kernel(x, mem, ffn1_norm, ffn1_w_gate_up, ffn1_w_down, mix_norm, mix_w_in, dn_conv_w, dn_a_log, dn_dt_bias,
           dn_out_norm, sg_norm_gain, sg_norm_bias, sg_w_spatial, sg_b_spatial, mix_w_out, xa_norm, xa_mem_norm,
           xa_w_q, xa_w_kv, xa_w_o, ffn2_norm, ffn2_w_gate_up, ffn2_w_down, final_norm):
    batch, seq, d = x.shape
    depth = ffn1_norm.shape[0]
    ffn1_gu, ffn1_d = _to_bf16(ffn1_w_gate_up), _to_bf16(ffn1_w_down)
    ffn2_gu, ffn2_d = _to_bf16(ffn2_w_gate_up), _to_bf16(ffn2_w_down)
    w_in, w_out = _mix_in_weights(mix_w_in), _to_bf16(mix_w_out)
    w_q, w_kv, w_o = _to_bf16(xa_w_q), _to_bf16(xa_w_kv), _to_bf16(xa_w_o)
    h = x.reshape(batch * seq, d)
    for i in range(depth):
        h = _ffn(h, ffn1_norm[i], ffn1_gu, ffn1_d, i)

        gate_params = jnp.zeros((8, LANES), F32)
        gate_params = gate_params.at[0, DN_HEADS:2 * DN_HEADS].set(dn_a_log[i])
        gate_params = gate_params.at[1, DN_HEADS:2 * DN_HEADS].set(dn_dt_bias[i])
        b_sp_rows = jnp.repeat(sg_b_spatial[i].T, SG_DIM, axis=1)
        qkv, z_gate, ba, swa_qkv, out_c = _proj(h, mix_norm[i], w_in, i, dn_conv_w[i], sg_norm_gain[i],
                                                sg_norm_bias[i], sg_w_spatial[i], b_sp_rows, seq)
        out_a = _deltanet(qkv, z_gate, ba, gate_params, dn_out_norm[i], batch)
        out_b = _swa(swa_qkv, batch)
        k, v = _mem_kv(mem, xa_mem_norm[i], w_kv, i)
        h = _mix_out_cross_attention(h, out_a, out_b, out_c, w_out, xa_norm[i], w_q, k, v, w_o, i, batch)
        h = _ffn(h, ffn2_norm[i], ffn2_gu, ffn2_d, i, final_gain=final_norm if i == depth - 1 else None)
    return h.reshape(batch, seq, d)
```

```python
import functools

import jax
import jax.numpy as jnp
from jax import lax
from jax.experimental import pallas as pl
from jax.experimental.pallas import tpu as pltpu

F32 = jnp.float32
BF16 = jnp.bfloat16
NORM_EPS = 1e-6
LANES = 128
SUBLANES = 8
VMEM_LIMIT = 56 * 1024 * 1024

DN_HEADS = 4
DN_DIM = 128
DN_WIDTH = DN_HEADS * DN_DIM
DN_CONV = 4
DN_CHUNK = 64
SWA_HEADS = 4
SWA_DIM = 64
SWA_WIDTH = SWA_HEADS * SWA_DIM
SWA_PATTERNS = ((128, 1), (512, 4), (2048, 16))
SWA_BLOCK = 128
SG_GROUPS = 4
SG_DIM = 64
SG_WIDTH = SG_GROUPS * SG_DIM
SG_CHUNK = 128
XA_HEADS = 4
NEG_BIG = -1e30


def _resident(shape):
    return pl.BlockSpec(shape, lambda *_: (0,) * len(shape), pipeline_mode=pl.Buffered(1))


def _layer_spec(stacked, layer):
    return pl.BlockSpec((None,) + stacked.shape[1:], lambda *_: (layer, 0, 0), pipeline_mode=pl.Buffered(1))


def _params(*sem):
    return pltpu.CompilerParams(dimension_semantics=sem, vmem_limit_bytes=VMEM_LIMIT)


def _rms(x, gain):
    ms = jnp.mean(x * x, axis=-1, keepdims=True)
    return x * lax.rsqrt(ms + NORM_EPS) * gain


def _dot(a, b):
    return jnp.dot(a, b, preferred_element_type=F32)


def _dot_nt(a, b):
    return lax.dot_general(a, b, (((1,), (1,)), ((), ())), preferred_element_type=F32)


def _dot_tn(a, b):
    return lax.dot_general(a, b, (((0,), (0,)), ((), ())), preferred_element_type=F32)


def _silu(x):
    return x * jax.nn.sigmoid(x)


def _cast_kernel(x_ref, o_ref):
    o_ref[...] = x_ref[...].astype(o_ref.dtype)


def _to_bf16(w, tr=256):
    depth, rows, cols = w.shape
    blk = pl.BlockSpec((None, tr, cols), lambda l, r: (l, r, 0))
    return pl.pallas_call(
        _cast_kernel,
        grid=(depth, rows // tr),
        in_specs=[blk],
        out_specs=blk,
        out_shape=jax.ShapeDtypeStruct(w.shape, BF16),
        compiler_params=_params("parallel", "parallel"),
        name="cast_bf16",
    )(w)


MIX_COLS = 3 * DN_WIDTH + DN_WIDTH + 3 * SWA_WIDTH + 2 * SG_WIDTH + LANES


def _cast_mix_in_kernel(x_ref, o_ref):
    o_b = 4 * DN_WIDTH
    o_swa = o_b + 2 * DN_HEADS
    x = x_ref[...]
    rest = x.shape[1] - o_swa
    o_ref[:, 0:o_b] = x[:, 0:o_b].astype(BF16)
    o_ref[:, o_b:o_b + rest] = x[:, o_swa:].astype(BF16)
    gates = jnp.concatenate([x[:, o_b:o_swa], jnp.zeros((x.shape[0], LANES - 2 * DN_HEADS), F32)], axis=1)
    o_ref[:, o_b + rest:] = gates.astype(BF16)


def _mix_in_weights(w_in, tr=256):
    depth, rows, cols = w_in.shape
    return pl.pallas_call(
        _cast_mix_in_kernel,
        grid=(depth, rows // tr),
        in_specs=[pl.BlockSpec((None, tr, cols), lambda l, r: (l, r, 0))],
        out_specs=pl.BlockSpec((None, tr, MIX_COLS), lambda l, r: (l, r, 0)),
        out_shape=jax.ShapeDtypeStruct((depth, rows, MIX_COLS), BF16),
        compiler_params=_params("parallel", "parallel"),
        name="cast_mix_in",
    )(w_in)


def _ffn_kernel(*refs, d_ff, chunk, final):
    if final:
        x_ref, g_ref, wgu_ref, wd_ref, fg_ref, o_ref, h_scr = refs
    else:
        x_ref, g_ref, wgu_ref, wd_ref, o_ref, h_scr = refs
    x = x_ref[...]
    n = _rms(x, g_ref[...]).astype(BF16)
    for c in range(d_ff // chunk):
        gate = _dot(n, wgu_ref[:, c * chunk:(c + 1) * chunk])
        up = _dot(n, wgu_ref[:, d_ff + c * chunk:d_ff + (c + 1) * chunk])
        h_scr[:, c * chunk:(c + 1) * chunk] = (_silu(gate) * up).astype(BF16)
    y = x + 0.5 * _dot(h_scr[...], wd_ref[...])
    if final:
        y = _rms(y, fg_ref[...])
    o_ref[...] = y


def _ffn(h, gain, w_gu, w_d, layer, final_gain=None, tm=1024, chunk=256):
    t, d = h.shape
    d_ff = w_d.shape[1]
    final = final_gain is not None
    row = pl.BlockSpec((tm, d), lambda i: (i, 0))
    vec = pl.BlockSpec((1, d), lambda i: (0, 0))
    in_specs = [row, vec, _layer_spec(w_gu, layer), _layer_spec(w_d, layer)]
    args = [h, gain.reshape(1, d), w_gu, w_d]
    if final:
        in_specs.append(vec)
        args.append(final_gain.reshape(1, d))
    return pl.pallas_call(
        functools.partial(_ffn_kernel, d_ff=d_ff, chunk=chunk, final=final),
        grid=(t // tm,),
        in_specs=in_specs,
        out_specs=row,
        out_shape=jax.ShapeDtypeStruct((t, d), F32),
        scratch_shapes=[pltpu.VMEM((tm, d_ff), BF16)],
        compiler_params=_params("parallel"),
        name="ffn_final" if final else "ffn",
    )(*args)


def _proj_kernel(x_ref, g_ref, w_ref, cw_ref, lg_ref, lb_ref, wsp_ref, bsp_ref,
                 qkv_ref, z_ref, ba_ref, swa_ref, c_ref, xs_ref, *, tm, tiles_per_seq):
    halo = SUBLANES
    piece = DN_CHUNK
    n = _rms(x_ref[...], g_ref[...]).astype(BF16)
    o_z = 3 * DN_WIDTH
    o_swa = o_z + DN_WIDTH
    o_sg = o_swa + 3 * SWA_WIDTH
    o_ba = o_sg + 2 * SG_WIDTH

    @pl.when(pl.program_id(0) % tiles_per_seq == 0)
    def _():
        xs_ref[0:halo, :] = jnp.zeros((halo, o_z), F32)

    cw = cw_ref[...]

    def stage(part):
        cols = slice(part * DN_WIDTH, (part + 1) * DN_WIDTH)
        xs_ref[halo:halo + tm, cols] = _dot(n, w_ref[:, cols])

    def dn_front(part):
        cols = slice(part * DN_WIDTH, (part + 1) * DN_WIDTH)
        for c in range(tm // piece):
            x = xs_ref[c * piece:c * piece + piece + halo, cols]
            acc = cw[DN_CONV - 1:DN_CONV, cols] * x[halo:halo + piece]
            for j in range(DN_CONV - 1):
                lo = halo - (DN_CONV - 1) + j
                acc = acc + cw[j:j + 1, cols] * x[lo:lo + piece]
            a = _silu(acc)
            rows = slice(c * piece, (c + 1) * piece)
            if part == 2:
                qkv_ref[rows, cols] = a
                continue
            for h in range(DN_HEADS):
                y = a[:, h * DN_DIM:(h + 1) * DN_DIM]
                inv = lax.rsqrt(jnp.sum(y * y, axis=-1, keepdims=True) + NORM_EPS)
                if part == 0:
                    inv = inv * (DN_DIM ** -0.5)
                qkv_ref[rows, part * DN_WIDTH + h * DN_DIM:part * DN_WIDTH + (h + 1) * DN_DIM] = y * inv

    stage(0)
    stage(1)
    dn_front(0)
    stage(2)
    dn_front(1)
    z_ref[...] = _silu(_dot(n, w_ref[:, o_z:o_swa]))
    swa_ref[...] = _dot(n, w_ref[:, o_swa:o_sg])
    dn_front(2)
    ba_ref[...] = _dot(n, w_ref[:, o_ba:o_ba + LANES])

    uv = jax.nn.gelu(_dot(n, w_ref[:, o_sg:o_ba]))
    u = uv[:, :SG_WIDTH]
    v = uv[:, SG_WIDTH:]
    mu = jnp.mean(v, axis=-1, keepdims=True)
    vc = v - mu
    var = jnp.mean(vc * vc, axis=-1, keepdims=True)
    vn = vc * lax.rsqrt(var + NORM_EPS) * lg_ref[...] + lb_ref[...]
    row = lax.broadcasted_iota(jnp.int32, (SG_CHUNK, SG_CHUNK), 0)
    col = lax.broadcasted_iota(jnp.int32, (SG_CHUNK, SG_CHUNK), 1)
    causal = row >= col
    low_lanes = col < SG_DIM
    w_causal = [jnp.where(causal, wsp_ref[g], 0.0).astype(BF16) for g in range(SG_GROUPS)]
    for c in range(tm // SG_CHUNK):
        rows = slice(c * SG_CHUNK, (c + 1) * SG_CHUNK)
        for p in range(SG_GROUPS // 2):
            lanes = slice(p * LANES, (p + 1) * LANES)
            vp = vn[rows, lanes]
            v_lo = jnp.where(low_lanes, vp, 0.0).astype(BF16)
            v_hi = jnp.where(low_lanes, 0.0, vp).astype(BF16)
            mixed = _dot(w_causal[2 * p], v_lo) + _dot(w_causal[2 * p + 1], v_hi) + bsp_ref[:, lanes]
            c_ref[rows, lanes] = (u[rows, lanes] * mixed).astype(BF16)

    xs_ref[0:halo, :] = xs_ref[tm:tm + halo, :]


def _proj(h, gain, w_all, layer, conv_w, sg_gain, sg_bias, w_sp, b_sp_rows, seq, tm=512):
    t, d = h.shape

    def row(w):
        return pl.BlockSpec((tm, w), lambda i: (i, 0))

    return pl.pallas_call(
        functools.partial(_proj_kernel, tm=tm, tiles_per_seq=seq // tm),
        grid=(t // tm,),
        in_specs=[row(d), pl.BlockSpec((1, d), lambda i: (0, 0)), _layer_spec(w_all, layer),
                  pl.BlockSpec((DN_CONV, 3 * DN_WIDTH), lambda i: (0, 0)),
                  pl.BlockSpec((1, SG_WIDTH), lambda i: (0, 0)),
                  pl.BlockSpec((1, SG_WIDTH), lambda i: (0, 0)),
                  _resident((SG_GROUPS, SG_CHUNK, SG_CHUNK)),
                  _resident((SG_CHUNK, SG_WIDTH))],
        out_specs=[row(3 * DN_WIDTH), row(DN_WIDTH), row(LANES), row(3 * SWA_WIDTH), row(SG_WIDTH)],
        out_shape=[jax.ShapeDtypeStruct((t, 3 * DN_WIDTH), F32),
                   jax.ShapeDtypeStruct((t, DN_WIDTH), F32),
                   jax.ShapeDtypeStruct((t, LANES), F32),
                   jax.ShapeDtypeStruct((t, 3 * SWA_WIDTH), F32),
                   jax.ShapeDtypeStruct((t, SG_WIDTH), BF16)],
        scratch_shapes=[pltpu.VMEM((tm + 8, 3 * DN_WIDTH), F32)],
        compiler_params=_params("arbitrary"),
        name="mix_proj",
    )(h, gain.reshape(1, d), w_all, conv_w, sg_gain.reshape(1, -1), sg_bias.reshape(1, -1), w_sp, b_sp_rows)


def _split3_bf16(a):
    hi = a.astype(BF16)
    r1 = a - hi.astype(F32)
    mid = r1.astype(BF16)
    lo = (r1 - mid.astype(F32)).astype(BF16)
    return hi, mid, lo


def _dot_exact_lhs(lhs16, rhs):
    hi, mid, lo = _split3_bf16(rhs)
    return _dot(lhs16, hi) + _dot(lhs16, mid) + _dot(lhs16, lo)


def _dn_kernel(qkv_ref, z_ref, ba_ref, gp_ref, og_ref, o_ref, st_ref, *, ts, batch):
    c_len = DN_CHUNK
    heads = range(DN_HEADS)
    seqs = range(batch)

    @pl.when(pl.program_id(0) == 0)
    def _():
        st_ref[...] = jnp.zeros_like(st_ref)

    neg_a = -jnp.exp(gp_ref[0:1, :])
    dt_bias = gp_ref[1:2, :]
    out_gain = og_ref[...]

    pairs = range(DN_HEADS // 2)
    row = lax.broadcasted_iota(jnp.int32, (c_len, LANES), 0)
    lane = lax.broadcasted_iota(jnp.int32, (c_len, LANES), 1)
    col = lane & (c_len - 1)
    low = lane < c_len
    tri_incl = row >= col
    tri_strict = row > col
    eye = (row == col).astype(F32)
    tri16 = tri_incl[:, :c_len].astype(BF16)
    ones16 = jnp.ones((c_len, LANES), BF16)
    base_mask = ((row >> 1) == (col >> 1)).astype(F32)
    level_masks = []
    sh = 1
    while (1 << sh) < c_len:
        level_masks.append((((row >> (sh + 1)) == (col >> (sh + 1))) & ((row >> sh) != (col >> sh))).astype(F32))
        sh += 1

    def block_diag(x):
        zero = jnp.zeros_like(x)
        return jnp.concatenate([jnp.where(low, x, zero), jnp.where(low, zero, x)], axis=0)

    def stack_diag(x0, x1):
        zero = jnp.zeros_like(x0)
        return jnp.concatenate([jnp.concatenate([x0, zero], axis=1), jnp.concatenate([zero, x1], axis=1)], axis=0)

    def prepare(chunks):
        slabs = [(j, s) for j in chunks for s in seqs]
        slab_ids = range(len(slabs))
        units = [(b, h) for b in slab_ids for h in heads]
        pair_units = [(b, p) for b in slab_ids for p in pairs]
        n_heads = len(heads)

        def rows_of(b):
            return slice(slabs[b][0] * c_len, (slabs[b][0] + 1) * c_len)

        beta_all, gc_all = [], []
        for b in slab_ids:
            ba = ba_ref[slabs[b][1], rows_of(b), :]
            beta_all.append(jax.nn.sigmoid(ba))
            g_all = neg_a * jax.nn.softplus(ba + dt_bias)
            gc_all.append(_dot_exact_lhs(tri16, g_all))

        def head_lanes(b, base, h):
            return qkv_ref[slabs[b][1], rows_of(b), base + h * DN_DIM:base + (h + 1) * DN_DIM]

        q = [head_lanes(b, 0, h) for b, h in units]
        k = [head_lanes(b, DN_WIDTH, h) for b, h in units]
        v = [head_lanes(b, 2 * DN_WIDTH, h) for b, h in units]
        beta = [beta_all[b][:, h:h + 1] for b, h in units]
        gcb = [jnp.broadcast_to(gc_all[b][:, n_heads + h:n_heads + h + 1], (c_len, LANES)) for b, h in units]
        g_last = [t[c_len - 1:c_len, :] for t in gcb]
        gcol = [jnp.where(low, gcb[b * n_heads + 2 * p], gcb[b * n_heads + 2 * p + 1]) for b, p in pair_units]
        gc_parts = [_split3_bf16(t) for t in gc_all]
        zero16 = jnp.zeros((c_len, LANES), BF16)
        grow = []
        for b, p in pair_units:
            acc = None
            for part in gc_parts[b]:
                picked = jnp.concatenate([jnp.where(lane == n_heads + 2 * p, part, zero16),
                                          jnp.where(lane == n_heads + 2 * p + 1, part, zero16)], axis=0)
                term = _dot_nt(ones16, picked)
                acc = term if acc is None else acc + term
            grow.append(acc)
        decay = [jnp.where(tri_incl, jnp.exp(jnp.where(tri_incl, gc - gr, 0.0)), 0.0) for gc, gr in zip(gcol, grow)]
        kb = [t * bb for t, bb in zip(k, beta)]

        def pair_of(xs, b, p):
            return xs[b * n_heads + 2 * p], xs[b * n_heads + 2 * p + 1]

        both = []
        for b, p in pair_units:
            kb0, kb1 = pair_of(kb, b, p)
            q0, q1 = pair_of(q, b, p)
            k0, k1 = pair_of(k, b, p)
            lhs = jnp.concatenate([jnp.concatenate([kb0, kb1], axis=1), jnp.concatenate([q0, q1], axis=1)], axis=0)
            both.append(_dot_nt(lhs.astype(BF16), stack_diag(k0.astype(BF16), k1.astype(BF16))))
        lower = [jnp.where(tri_strict, t[:c_len] * d, 0.0) for t, d in zip(both, decay)]
        a_qk = [jnp.where(tri_incl, t[c_len:] * d, 0.0).astype(BF16) for t, d in zip(both, decay)]
        t_inv = [eye - lo * base_mask for lo in lower]
        for m in level_masks:
            t16 = [t.astype(BF16) for t in t_inv]
            left = [_dot(t, block_diag((lo * m).astype(BF16))).astype(BF16) for t, lo in zip(t16, lower)]
            t_inv = [t - _dot(l, block_diag(t_b)) for t, l, t_b in zip(t_inv, left, t16)]
        t16 = [t.astype(BF16) for t in t_inv]
        egc = [jnp.exp(t) for t in gcb]
        vb = [(vv * bb).astype(BF16) for vv, bb in zip(v, beta)]
        kbe = [(kk * e).astype(BF16) for kk, e in zip(kb, egc)]
        uw = []
        for (b, p), t in zip(pair_units, t16):
            vb0, vb1 = pair_of(vb, b, p)
            kbe0, kbe1 = pair_of(kbe, b, p)
            uw.append(_dot(t, stack_diag(jnp.concatenate([vb0, kbe0], axis=1), jnp.concatenate([vb1, kbe1], axis=1))))
        u, w = [], []
        for t in uw:
            for hh in range(2):
                u.append(t[:, 2 * hh * DN_DIM:(2 * hh + 1) * DN_DIM])
                w.append(t[:, (2 * hh + 1) * DN_DIM:(2 * hh + 2) * DN_DIM])
        k_tail = [(kk * jnp.exp(gl - gc)).astype(BF16) for kk, gl, gc in zip(k, g_last, gcb)]
        wq = [jnp.concatenate([ww, qq * e], axis=0).astype(BF16) for ww, qq, e in zip(w, q, egc)]
        s_decay = [jnp.exp(gl) for gl in g_last]
        return u, wq, a_qk, k_tail, s_decay

    def advance(j, prepared, state):
        rows = slice(j * c_len, (j + 1) * c_len)
        units = [(b, h) for b in seqs for h in heads]
        n_units = len(units)
        u, wq, k_tail, s_decay = [t[j * n_units:(j + 1) * n_units] for t in (prepared[0], prepared[1], prepared[3], prepared[4])]
        a_qk = prepared[2][j * n_units // 2:(j + 1) * n_units // 2]
        ws = [_dot(t, s.astype(BF16)) for t, s in zip(wq, state)]
        vn16 = [(uu - t[:c_len]).astype(BF16) for uu, t in zip(u, ws)]
        new_state = [s * d + _dot_tn(kt, vn) for s, d, kt, vn in zip(state, s_decay, k_tail, vn16)]
        intra = [_dot(aq, stack_diag(vn16[2 * i], vn16[2 * i + 1])) for i, aq in enumerate(a_qk)]
        o = [t[c_len:] + intra[i // 2][:, (i % 2) * DN_DIM:(i % 2 + 1) * DN_DIM] for i, t in enumerate(ws)]
        o = [t * lax.rsqrt(jnp.mean(t * t, axis=-1, keepdims=True) + NORM_EPS) * out_gain for t in o]
        for (b, h), t in zip(units, o):
            lanes = slice(h * DN_DIM, (h + 1) * DN_DIM)
            o_ref[b, rows, lanes] = (t * z_ref[b, rows, lanes]).astype(o_ref.dtype)
        return new_state

    n_units = len(seqs) * len(heads)
    n_chunks = ts // c_len
    state = [st_ref[i] for i in range(n_units)]
    prepared = prepare(range(n_chunks))
    for j in range(n_chunks):
        state = advance(j, prepared, state)
    for i in range(n_units):
        st_ref[i] = state[i]


def _deltanet(qkv, z_gate, ba, gate_params, out_gain, batch, ts=256):
    t = qkv.shape[0]
    seq = t // batch

    def rows(w):
        return pl.BlockSpec((batch, ts, w), lambda s: (0, s, 0))

    out = pl.pallas_call(
        functools.partial(_dn_kernel, ts=ts, batch=batch),
        grid=(seq // ts,),
        in_specs=[rows(3 * DN_WIDTH), rows(DN_WIDTH), rows(LANES),
                  pl.BlockSpec((8, LANES), lambda s: (0, 0)),
                  pl.BlockSpec((1, DN_DIM), lambda s: (0, 0))],
        out_specs=rows(DN_WIDTH),
        out_shape=jax.ShapeDtypeStruct((batch, seq, DN_WIDTH), BF16),
        scratch_shapes=[pltpu.VMEM((batch * DN_HEADS, DN_DIM, DN_DIM), F32)],
        compiler_params=_params("arbitrary"),
        name="deltanet",
    )(qkv.reshape(batch, seq, -1), z_gate.reshape(batch, seq, -1), ba.reshape(batch, seq, -1), gate_params,
      out_gain.reshape(1, DN_DIM))
    return out.reshape(t, DN_WIDTH)


def _swa_bias(slopes, dilation):
    blk = SWA_BLOCK
    shape = (2 * blk, 2 * blk)
    row = lax.broadcasted_iota(jnp.int32, shape, 0)
    col = lax.broadcasted_iota(jnp.int32, shape, 1)
    rel = (row & (blk - 1)) + blk - col
    slope = jnp.where(row < blk, slopes[0] * dilation, slopes[1] * dilation)
    return jnp.where((rel >= 0) & (rel <= blk), -slope * rel.astype(F32), NEG_BIG)


def _swa_kernel(q0_ref, q1_ref, k0_ref, k1_ref, v0_ref, v1_ref, pk0_ref, pk1_ref, pv0_ref, pv1_ref,
                o_ref, acc_scr, lse_scr, *, tq, group):
    blk = SWA_BLOCK
    q_refs = (q0_ref, q1_ref)
    kv_refs = ((k0_ref, k1_ref), (v0_ref, v1_ref))
    prev_refs = ((pk0_ref, pk1_ref), (pv0_ref, pv1_ref))
    pairs = range(SWA_HEADS // 2)
    have_prev = pl.program_id(1) > 0

    low = lax.broadcasted_iota(jnp.int32, (blk, LANES), 1) < SWA_DIM
    same_block = lax.broadcasted_iota(jnp.int32, (2 * blk, 2 * blk), 1) >= blk
    slopes = [2.0 ** (-8.0 * (h + 1) / SWA_HEADS) for h in range(SWA_HEADS)]

    def rows_of(start, count, dilation):
        return pl.ds(start, count, stride=dilation) if dilation > 1 else pl.ds(start, count)

    def run_group(pi, dilation, bias, starts, crosses_tile):
        units = range(len(starts))
        q_rows = [rows_of(s, blk, dilation) for s in starts]
        up = [(u, p) for u in units for p in pairs]
        q = [(q_refs[p][q_rows[u], :] * (SWA_DIM ** -0.5)).astype(BF16) for u, p in up]
        zero = jnp.zeros_like(q[0])
        q2 = [jnp.concatenate([jnp.where(low, t, zero), jnp.where(low, zero, t)], axis=0) for t in q]

        def keys_or_values(which, u, p):
            if crosses_tile[u]:
                before = prev_refs[which][p][rows_of(tq + starts[u] - dilation * blk, blk, dilation), :]
                return jnp.concatenate([before, kv_refs[which][p][q_rows[u], :]], axis=0).astype(BF16)
            return kv_refs[which][p][rows_of(starts[u] - dilation * blk, 2 * blk, dilation), :].astype(BF16)

        k2 = [keys_or_values(0, u, p) for u, p in up]
        v2 = [keys_or_values(1, u, p) for u, p in up]

        def unit_bias(u, p):
            if not crosses_tile[u]:
                return bias[p]
            return jnp.where(jnp.logical_or(same_block, have_prev), bias[p], NEG_BIG)

        s = [_dot_nt(a, b) + unit_bias(u, p) for (u, p), a, b in zip(up, q2, k2)]
        m = [jnp.max(t, axis=-1, keepdims=True) for t in s]
        e = [jnp.exp(t - mm) for t, mm in zip(s, m)]
        den = [jnp.sum(t, axis=-1, keepdims=True) for t in e]
        pv = [_dot(t.astype(BF16), b) for t, b in zip(e, v2)]
        for (u, p), mm, dd, o in zip(up, m, den, pv):
            inv = 1.0 / dd
            lse = mm + jnp.log(dd)
            acc_scr[pi, p, q_rows[u], :] = jnp.where(low, o[:blk] * inv[:blk], o[blk:] * inv[blk:])
            lse_scr[pi, p, q_rows[u], :] = jnp.where(low, lse[:blk], lse[blk:])

    for pi, (_, dilation) in enumerate(SWA_PATTERNS):
        bias = [_swa_bias(slopes[2 * p:2 * p + 2], dilation) for p in pairs]
        per_residue = tq // (dilation * blk)
        if per_residue >= group:
            n_trips = per_residue // group
            for r in range(dilation):
                def trip(g, carry, r=r, pi=pi, dilation=dilation, bias=bias):
                    starts = [r + dilation * blk * (g * group + u) for u in range(group)]
                    first = isinstance(g, int) and g == 0
                    run_group(pi, dilation, bias, starts, [first] + [False] * (group - 1))
                    return carry
                trip(0, 0)
                if n_trips > 1:
                    lax.fori_loop(1, n_trips, trip, 0)
        else:
            assert per_residue == 1 and dilation % group == 0
            for r0 in range(0, dilation, group):
                run_group(pi, dilation, bias, [r0 + u for u in range(group)], [True] * group)

    merge_rows = 2 * SWA_BLOCK
    n_pat = len(SWA_PATTERNS)
    for c in range(tq // merge_rows):
        rows = slice(c * merge_rows, (c + 1) * merge_rows)
        for p in pairs:
            lses = [lse_scr[pi, p, rows, :] for pi in range(n_pat)]
            top = functools.reduce(jnp.maximum, lses)
            es = [jnp.exp(l - top) for l in lses]
            inv = 1.0 / functools.reduce(lambda a, b: a + b, es)
            merged = functools.reduce(lambda a, b: a + b, [es[pi] * acc_scr[pi, p, rows, :] for pi in range(n_pat)])
            o_ref[rows, p * LANES:(p + 1) * LANES] = (merged * inv).astype(o_ref.dtype)


def _swa(swa_qkv, batch, tq=2048, group=4):
    t = swa_qkv.shape[0]
    nj = t // batch // tq
    n_tiles = SWA_WIDTH // LANES
    n_pat = len(SWA_PATTERNS)

    def cur(c):
        return pl.BlockSpec((tq, LANES), lambda b, j: (b * nj + j, c))

    def prev(c):
        return pl.BlockSpec((tq, LANES), lambda b, j: (b * nj + jnp.maximum(j - 1, 0), c))

    cols = range(3 * n_tiles)
    kv_cols = range(n_tiles, 3 * n_tiles)
    return pl.pallas_call(
        functools.partial(_swa_kernel, tq=tq, group=group),
        grid=(batch, nj),
        in_specs=[cur(c) for c in cols] + [prev(c) for c in kv_cols],
        out_specs=pl.BlockSpec((tq, SWA_WIDTH), lambda b, j: (b * nj + j, 0)),
        out_shape=jax.ShapeDtypeStruct((t, SWA_WIDTH), BF16),
        scratch_shapes=[pltpu.VMEM((n_pat, n_tiles, tq, LANES), F32),
                        pltpu.VMEM((n_pat, n_tiles, tq, LANES), F32)],
        compiler_params=_params("parallel", "parallel"),
        name="swa",
    )(*([swa_qkv] * (len(cols) + len(kv_cols))))


def _kv_kernel(mem_ref, g_ref, w_ref, k_ref, v_ref):
    n = _rms(mem_ref[...], g_ref[...]).astype(BF16)
    d = k_ref.shape[-1]
    k_ref[...] = _dot(n, w_ref[:, :d]).astype(BF16)
    v_ref[...] = _dot(n, w_ref[:, d:]).astype(BF16)


def _mem_kv(mem, gain, w_kv, layer):
    b, m, d = mem.shape
    blk = pl.BlockSpec((None, m, d), lambda i: (i, 0, 0))
    return pl.pallas_call(
        _kv_kernel,
        grid=(b,),
        in_specs=[blk, pl.BlockSpec((1, d), lambda i: (0, 0)), _layer_spec(w_kv, layer)],
        out_specs=[blk, blk],
        out_shape=[jax.ShapeDtypeStruct((b, m, d), BF16)] * 2,
        compiler_params=_params("parallel"),
        name="xa_kv",
    )(mem, gain.reshape(1, d), w_kv)


def _mix_xa_kernel(h_ref, a_ref, b_ref, c_ref, wm_ref, g_ref, wq_ref, k_ref, v_ref, wo_ref, o_ref, att_scr):
    x = h_ref[...] + _dot(a_ref[...], wm_ref[0:DN_WIDTH, :])
    x = x + _dot(b_ref[...], wm_ref[DN_WIDTH:DN_WIDTH + SWA_WIDTH, :])
    x = x + _dot(c_ref[...], wm_ref[DN_WIDTH + SWA_WIDTH:, :])
    n = _rms(x, g_ref[...]).astype(BF16)
    d = x.shape[-1]
    dh = d // XA_HEADS
    q = _dot(n, wq_ref[...]).astype(BF16)
    lanes = [slice(h * dh, (h + 1) * dh) for h in range(XA_HEADS)]
    s = [_dot_nt(q[:, ln], k_ref[:, ln]) * (dh ** -0.5) for ln in lanes]
    m = [jnp.max(t, axis=-1, keepdims=True) for t in s]
    p = [jnp.exp(t - mm) for t, mm in zip(s, m)]
    p = [t * (1.0 / jnp.sum(t, axis=-1, keepdims=True)) for t in p]
    for t, ln in zip(p, lanes):
        att_scr[:, ln] = _dot(t.astype(BF16), v_ref[:, ln]).astype(BF16)
    o_ref[...] = x + _dot(att_scr[...], wo_ref[...])


def _mix_out_cross_attention(h, out_a, out_b, out_c, w_out, gain, w_q, k, v, w_o, layer, batch, tm=1024):
    t, d = h.shape
    ns = t // batch // tm
    m = k.shape[1]

    def row(w):
        return pl.BlockSpec((tm, w), lambda b, s: (b * ns + s, 0))

    kv = pl.BlockSpec((None, m, d), lambda b, s: (b, 0, 0))
    return pl.pallas_call(
        _mix_xa_kernel,
        grid=(batch, ns),
        in_specs=[row(d), row(DN_WIDTH), row(SWA_WIDTH), row(SG_WIDTH), _layer_spec(w_out, layer),
                  pl.BlockSpec((1, d), lambda b, s: (0, 0)), _layer_spec(w_q, layer), kv, kv, _layer_spec(w_o, layer)],
        out_specs=row(d),
        out_shape=jax.ShapeDtypeStruct((t, d), F32),
        scratch_shapes=[pltpu.VMEM((tm, d), BF16)],
        compiler_params=_params("parallel", "parallel"),
        name="mix_out_cross_attn",
    )(h, out_a, out_b, out_c, w_out, gain.reshape(1, d), w_q, k, v, w_o)


def kernel(x, mem, ffn1_norm, ffn1_w_gate_up, ffn1_w_down, mix_norm, mix_w_in, dn_conv_w, dn_a_log, dn_dt_bias,
           dn_out_norm, sg_norm_gain, sg_norm_bias, sg_w_spatial, sg_b_spatial, mix_w_out, xa_norm, xa_mem_norm,
           xa_w_q, xa_w_kv, xa_w_o, ffn2_norm, ffn2_w_gate_up, ffn2_w_down, final_norm):
    batch, seq, d = x.shape
    depth = ffn1_norm.shape[0]
    ffn1_gu, ffn1_d = _to_bf16(ffn1_w_gate_up), _to_bf16(ffn1_w_down)
    ffn2_gu, ffn2_d = _to_bf16(ffn2_w_gate_up), _to_bf16(ffn2_w_down)
    w_in, w_out = _mix_in_weights(mix_w_in), _to_bf16(mix_w_out)
    w_q, w_kv, w_o = _to_bf16(xa_w_q), _to_bf16(xa_w_kv), _to_bf16(xa_w_o)
    h = x.reshape(batch * seq, d)
    for i in range(depth):
        h = _ffn(h, ffn1_norm[i], ffn1_gu, ffn1_d, i)

        gate_params = jnp.zeros((8, LANES), F32)
        gate_params = gate_params.at[0, DN_HEADS:2 * DN_HEADS].set(dn_a_log[i])
        gate_params = gate_params.at[1, DN_HEADS:2 * DN_HEADS].set(dn_dt_bias[i])
        b_sp_rows = jnp.repeat(sg_b_spatial[i].T, SG_DIM, axis=1)
        qkv, z_gate, ba, swa_qkv, out_c = _proj(h, mix_norm[i], w_in, i, dn_conv_w[i], sg_norm_gain[i],
                                                sg_norm_bias[i], sg_w_spatial[i], b_sp_rows, seq)
        out_a = _deltanet(qkv, z_gate, ba, gate_params, dn_out_norm[i], batch)
        out_b = _swa(swa_qkv, batch)
        k, v = _mem_kv(mem, xa_mem_norm[i], w_kv, i)
        h = _mix_out_cross_attention(h, out_a, out_b, out_c, w_out, xa_norm[i], w_q, k, v, w_o, i, batch)
        h = _ffn(h, ffn2_norm[i], ffn2_gu, ffn2_d, i, final_gain=final_norm if i == depth - 1 else None)
    return h.reshape(batch, seq, d)
```

```python
import functools

import jax
import jax.numpy as jnp
from jax import lax
from jax.experimental import pallas as pl
from jax.experimental.pallas import tpu as pltpu

F32 = jnp.float32
BF16 = jnp.bfloat16
NORM_EPS = 1e-6
LANES = 128
SUBLANES = 8
VMEM_LIMIT = 56 * 1024 * 1024

DN_HEADS = 4
DN_DIM = 128
DN_WIDTH = DN_HEADS * DN_DIM
DN_CONV = 4
DN_CHUNK = 64
SWA_HEADS = 4
SWA_DIM = 64
SWA_WIDTH = SWA_HEADS * SWA_DIM
SWA_PATTERNS = ((128, 1), (512, 4), (2048, 16))
SWA_BLOCK = 128
SG_GROUPS = 4
SG_DIM = 64
SG_WIDTH = SG_GROUPS * SG_DIM
SG_CHUNK = 128
XA_HEADS = 4
NEG_BIG = -1e30


def _resident(shape):
    return pl.BlockSpec(shape, lambda *_: (0,) * len(shape), pipeline_mode=pl.Buffered(1))


def _layer_spec(stacked, layer):
    return pl.BlockSpec((None,) + stacked.shape[1:], lambda *_: (layer, 0, 0), pipeline_mode=pl.Buffered(1))


def _params(*sem):
    return pltpu.CompilerParams(dimension_semantics=sem, vmem_limit_bytes=VMEM_LIMIT)


def _rms(x, gain):
    ms = jnp.mean(x * x, axis=-1, keepdims=True)
    return x * lax.rsqrt(ms + NORM_EPS) * gain


def _dot(a, b):
    return jnp.dot(a, b, preferred_element_type=F32)


def _dot_nt(a, b):
    return lax.dot_general(a, b, (((1,), (1,)), ((), ())), preferred_element_type=F32)


def _dot_tn(a, b):
    return lax.dot_general(a, b, (((0,), (0,)), ((), ())), preferred_element_type=F32)


def _silu(x):
    return x * jax.nn.sigmoid(x)


def _cast_kernel(x_ref, o_ref):
    o_ref[...] = x_ref[...].astype(o_ref.dtype)


def _to_bf16(w, tr=256):
    depth, rows, cols = w.shape
    blk = pl.BlockSpec((None, tr, cols), lambda l, r: (l, r, 0))
    return pl.pallas_call(
        _cast_kernel,
        grid=(depth, rows // tr),
        in_specs=[blk],
        out_specs=blk,
        out_shape=jax.ShapeDtypeStruct(w.shape, BF16),
        compiler_params=_params("parallel", "parallel"),
        name="cast_bf16",
    )(w)


MIX_COLS = 3 * DN_WIDTH + DN_WIDTH + 3 * SWA_WIDTH + 2 * SG_WIDTH + LANES


def _cast_mix_in_kernel(x_ref, o_ref):
    o_b = 4 * DN_WIDTH
    o_swa = o_b + 2 * DN_HEADS
    x = x_ref[...]
    rest = x.shape[1] - o_swa
    o_ref[:, 0:o_b] = x[:, 0:o_b].astype(BF16)
    o_ref[:, o_b:o_b + rest] = x[:, o_swa:].astype(BF16)
    gates = jnp.concatenate([x[:, o_b:o_swa], jnp.zeros((x.shape[0], LANES - 2 * DN_HEADS), F32)], axis=1)
    o_ref[:, o_b + rest:] = gates.astype(BF16)


def _mix_in_weights(w_in, tr=256):
    depth, rows, cols = w_in.shape
    return pl.pallas_call(
        _cast_mix_in_kernel,
        grid=(depth, rows // tr),
        in_specs=[pl.BlockSpec((None, tr, cols), lambda l, r: (l, r, 0))],
        out_specs=pl.BlockSpec((None, tr, MIX_COLS), lambda l, r: (l, r, 0)),
        out_shape=jax.ShapeDtypeStruct((depth, rows, MIX_COLS), BF16),
        compiler_params=_params("parallel", "parallel"),
        name="cast_mix_in",
    )(w_in)


def _ffn_kernel(*refs, d_ff, chunk, final):
    if final:
        x_ref, g_ref, wgu_ref, wd_ref, fg_ref, o_ref, h_scr = refs
    else:
        x_ref, g_ref, wgu_ref, wd_ref, o_ref, h_scr = refs
    x = x_ref[...]
    n = _rms(x, g_ref[...]).astype(BF16)
    for c in range(d_ff // chunk):
        gate = _dot(n, wgu_ref[:, c * chunk:(c + 1) * chunk])
        up = _dot(n, wgu_ref[:, d_ff + c * chunk:d_ff + (c + 1) * chunk])
        h_scr[:, c * chunk:(c + 1) * chunk] = (_silu(gate) * up).astype(BF16)
    y = x + 0.5 * _dot(h_scr[...], wd_ref[...])
    if final:
        y = _rms(y, fg_ref[...])
    o_ref[...] = y


def _ffn(h, gain, w_gu, w_d, layer, final_gain=None, tm=1024, chunk=256):
    t, d = h.shape
    d_ff = w_d.shape[1]
    final = final_gain is not None
    row = pl.BlockSpec((tm, d), lambda i: (i, 0))
    vec = pl.BlockSpec((1, d), lambda i: (0, 0))
    in_specs = [row, vec, _layer_spec(w_gu, layer), _layer_spec(w_d, layer)]
    args = [h, gain.reshape(1, d), w_gu, w_d]
    if final:
        in_specs.append(vec)
        args.append(final_gain.reshape(1, d))
    return pl.pallas_call(
        functools.partial(_ffn_kernel, d_ff=d_ff, chunk=chunk, final=final),
        grid=(t // tm,),
        in_specs=in_specs,
        out_specs=row,
        out_shape=jax.ShapeDtypeStruct((t, d), F32),
        scratch_shapes=[pltpu.VMEM((tm, d_ff), BF16)],
        compiler_params=_params("parallel"),
        name="ffn_final" if final else "ffn",
    )(*args)


def _proj_kernel(x_ref, g_ref, w_ref, cw_ref, lg_ref, lb_ref, wsp_ref, bsp_ref,
                 qkv_ref, z_ref, ba_ref, swa_ref, c_ref, xs_ref, *, tm, tiles_per_seq):
    halo = SUBLANES
    piece = DN_CHUNK
    n = _rms(x_ref[...], g_ref[...]).astype(BF16)
    o_z = 3 * DN_WIDTH
    o_swa = o_z + DN_WIDTH
    o_sg = o_swa + 3 * SWA_WIDTH
    o_ba = o_sg + 2 * SG_WIDTH

    n_tiles = o_z // LANES

    @pl.when(pl.program_id(0) % tiles_per_seq == 0)
    def _():
        xs_ref[:, 0:halo, :] = jnp.zeros((n_tiles, halo, LANES), F32)

    cw = cw_ref[...]
    tiles_per_part = DN_WIDTH // LANES

    def stage(part):
        y = _dot(n, w_ref[:, part * DN_WIDTH:(part + 1) * DN_WIDTH])
        for t in range(tiles_per_part):
            xs_ref[part * tiles_per_part + t, halo:halo + tm, :] = y[:, t * LANES:(t + 1) * LANES]

    def dn_front(part):
        groups = piece // SUBLANES
        for c in range(tm // piece):
            r0 = c * piece
            for t in range(tiles_per_part):
                tile = part * tiles_per_part + t
                lanes = slice(tile * LANES, (tile + 1) * LANES)
                taps = {d: xs_ref[tile, pl.ds(halo + r0 + d, groups, stride=SUBLANES), :]
                        for d in range(-(DN_CONV - 1), SUBLANES)}
                for r in range(SUBLANES):
                    acc = cw[DN_CONV - 1:DN_CONV, lanes] * taps[r]
                    for j in range(DN_CONV - 1):
                        acc = acc + cw[j:j + 1, lanes] * taps[r - (DN_CONV - 1) + j]
                    y = _silu(acc)
                    if part != 2:
                        inv = lax.rsqrt(jnp.sum(y * y, axis=-1, keepdims=True) + NORM_EPS)
                        if part == 0:
                            inv = inv * (DN_DIM ** -0.5)
                        y = y * inv
                    qkv_ref[r0 + r * groups:r0 + (r + 1) * groups, lanes] = y

    stage(0)
    stage(1)
    dn_front(0)
    stage(2)
    dn_front(1)
    z_ref[...] = _silu(_dot(n, w_ref[:, o_z:o_swa]))
    swa_ref[...] = _dot(n, w_ref[:, o_swa:o_sg])
    dn_front(2)
    ba_ref[...] = _dot(n, w_ref[:, o_ba:o_ba + LANES])

    uv = jax.nn.gelu(_dot(n, w_ref[:, o_sg:o_ba]))
    u = uv[:, :SG_WIDTH]
    v = uv[:, SG_WIDTH:]
    mu = jnp.mean(v, axis=-1, keepdims=True)
    vc = v - mu
    var = jnp.mean(vc * vc, axis=-1, keepdims=True)
    vn = vc * lax.rsqrt(var + NORM_EPS) * lg_ref[...] + lb_ref[...]
    row = lax.broadcasted_iota(jnp.int32, (SG_CHUNK, SG_CHUNK), 0)
    col = lax.broadcasted_iota(jnp.int32, (SG_CHUNK, SG_CHUNK), 1)
    causal = row >= col
    low_lanes = col < SG_DIM
    w_causal = [jnp.where(causal, wsp_ref[g], 0.0).astype(BF16) for g in range(SG_GROUPS)]
    for c in range(tm // SG_CHUNK):
        rows = slice(c * SG_CHUNK, (c + 1) * SG_CHUNK)
        for p in range(SG_GROUPS // 2):
            lanes = slice(p * LANES, (p + 1) * LANES)
            vp = vn[rows, lanes]
            v_lo = jnp.where(low_lanes, vp, 0.0).astype(BF16)
            v_hi = jnp.where(low_lanes, 0.0, vp).astype(BF16)
            mixed = _dot(w_causal[2 * p], v_lo) + _dot(w_causal[2 * p + 1], v_hi) + bsp_ref[:, lanes]
            c_ref[rows, lanes] = (u[rows, lanes] * mixed).astype(BF16)

    xs_ref[:, 0:halo, :] = xs_ref[:, tm:tm + halo, :]


def _proj(h, gain, w_all, layer, conv_w, sg_gain, sg_bias, w_sp, b_sp_rows, seq, tm=512):
    t, d = h.shape

    def row(w):
        return pl.BlockSpec((tm, w), lambda i: (i, 0))

    return pl.pallas_call(
        functools.partial(_proj_kernel, tm=tm, tiles_per_seq=seq // tm),
        grid=(t // tm,),
        in_specs=[row(d), pl.BlockSpec((1, d), lambda i: (0, 0)), _layer_spec(w_all, layer),
                  pl.BlockSpec((DN_CONV, 3 * DN_WIDTH), lambda i: (0, 0)),
                  pl.BlockSpec((1, SG_WIDTH), lambda i: (0, 0)),
                  pl.BlockSpec((1, SG_WIDTH), lambda i: (0, 0)),
                  _resident((SG_GROUPS, SG_CHUNK, SG_CHUNK)),
                  _resident((SG_CHUNK, SG_WIDTH))],
        out_specs=[row(3 * DN_WIDTH), row(DN_WIDTH), row(LANES), row(3 * SWA_WIDTH), row(SG_WIDTH)],
        out_shape=[jax.ShapeDtypeStruct((t, 3 * DN_WIDTH), F32),
                   jax.ShapeDtypeStruct((t, DN_WIDTH), F32),
                   jax.ShapeDtypeStruct((t, LANES), F32),
                   jax.ShapeDtypeStruct((t, 3 * SWA_WIDTH), F32),
                   jax.ShapeDtypeStruct((t, SG_WIDTH), BF16)],
        scratch_shapes=[pltpu.VMEM((3 * DN_WIDTH // LANES, tm + SUBLANES, LANES), F32)],
        compiler_params=_params("arbitrary"),
        name="mix_proj",
    )(h, gain.reshape(1, d), w_all, conv_w, sg_gain.reshape(1, -1), sg_bias.reshape(1, -1), w_sp, b_sp_rows)


def _split3_bf16(a):
    hi = a.astype(BF16)
    r1 = a - hi.astype(F32)
    mid = r1.astype(BF16)
    lo = (r1 - mid.astype(F32)).astype(BF16)
    return hi, mid, lo


def _dot_exact_lhs(lhs16, rhs):
    hi, mid, lo = _split3_bf16(rhs)
    return _dot(lhs16, hi) + _dot(lhs16, mid) + _dot(lhs16, lo)


def _dn_kernel(qkv_ref, z_ref, ba_ref, gp_ref, og_ref, o_ref, st_ref, out_scr, *, ts, batch):
    c_len = DN_CHUNK
    heads = range(DN_HEADS)
    seqs = range(batch)

    @pl.when(pl.program_id(0) == 0)
    def _():
        st_ref[...] = jnp.zeros_like(st_ref)

    neg_a = -jnp.exp(gp_ref[0:1, :])
    dt_bias = gp_ref[1:2, :]
    out_gain = og_ref[...]

    pairs = range(DN_HEADS // 2)
    groups = c_len // SUBLANES

    def time_of(pos):
        return (pos % SUBLANES) * groups + pos // SUBLANES

    pos = lax.broadcasted_iota(jnp.int32, (c_len, LANES), 0)
    lane = lax.broadcasted_iota(jnp.int32, (c_len, LANES), 1)
    low = lane < c_len
    row = time_of(pos)
    col = time_of(lane & (c_len - 1))
    tri_incl = row >= col
    tri_strict = row > col
    eye = (row == col).astype(F32)
    tri16 = tri_incl[:, :c_len].astype(BF16)
    ones16 = jnp.ones((c_len, LANES), BF16)
    base_mask = ((row >> 1) == (col >> 1)).astype(F32)
    level_masks = []
    sh = 1
    while (1 << sh) < c_len:
        level_masks.append((((row >> (sh + 1)) == (col >> (sh + 1))) & ((row >> sh) != (col >> sh))).astype(F32))
        sh += 1

    def block_diag(x):
        zero = jnp.zeros_like(x)
        return jnp.concatenate([jnp.where(low, x, zero), jnp.where(low, zero, x)], axis=0)

    def stack_diag(x0, x1):
        zero = jnp.zeros_like(x0)
        return jnp.concatenate([jnp.concatenate([x0, zero], axis=1), jnp.concatenate([zero, x1], axis=1)], axis=0)

    def prepare(chunks):
        slabs = [(j, s) for j in chunks for s in seqs]
        slab_ids = range(len(slabs))
        units = [(b, h) for b in slab_ids for h in heads]
        pair_units = [(b, p) for b in slab_ids for p in pairs]
        n_heads = len(heads)

        def rows_of(b):
            return slice(slabs[b][0] * c_len, (slabs[b][0] + 1) * c_len)

        beta_all, gc_all = [], []
        for b in slab_ids:
            start = slabs[b][0] * c_len
            ba = jnp.concatenate([ba_ref[slabs[b][1], pl.ds(start + r, groups, stride=SUBLANES), :]
                                  for r in range(SUBLANES)], axis=0)
            beta_all.append(jax.nn.sigmoid(ba))
            g_all = neg_a * jax.nn.softplus(ba + dt_bias)
            gc_all.append(_dot_exact_lhs(tri16, g_all))

        def head_lanes(b, base, h):
            return qkv_ref[slabs[b][1], rows_of(b), base + h * DN_DIM:base + (h + 1) * DN_DIM]

        q = [head_lanes(b, 0, h) for b, h in units]
        k = [head_lanes(b, DN_WIDTH, h) for b, h in units]
        v = [head_lanes(b, 2 * DN_WIDTH, h) for b, h in units]
        beta = [beta_all[b][:, h:h + 1] for b, h in units]
        gcb = [jnp.broadcast_to(gc_all[b][:, n_heads + h:n_heads + h + 1], (c_len, LANES)) for b, h in units]
        g_last = [t[c_len - 1:c_len, :] for t in gcb]
        gcol = [jnp.where(low, gcb[b * n_heads + 2 * p], gcb[b * n_heads + 2 * p + 1]) for b, p in pair_units]
        gc_parts = [_split3_bf16(t) for t in gc_all]
        zero16 = jnp.zeros((c_len, LANES), BF16)
        grow = []
        for b, p in pair_units:
            acc = None
            for part in gc_parts[b]:
                picked = jnp.concatenate([jnp.where(lane == n_heads + 2 * p, part, zero16),
                                          jnp.where(lane == n_heads + 2 * p + 1, part, zero16)], axis=0)
                term = _dot_nt(ones16, picked)
                acc = term if acc is None else acc + term
            grow.append(acc)
        decay = [jnp.where(tri_incl, jnp.exp(jnp.where(tri_incl, gc - gr, 0.0)), 0.0) for gc, gr in zip(gcol, grow)]
        kb = [t * bb for t, bb in zip(k, beta)]

        def pair_of(xs, b, p):
            return xs[b * n_heads + 2 * p], xs[b * n_heads + 2 * p + 1]

        both = []
        for b, p in pair_units:
            kb0, kb1 = pair_of(kb, b, p)
            q0, q1 = pair_of(q, b, p)
            k0, k1 = pair_of(k, b, p)
            lhs = jnp.concatenate([jnp.concatenate([kb0, kb1], axis=1), jnp.concatenate([q0, q1], axis=1)], axis=0)
            both.append(_dot_nt(lhs.astype(BF16), stack_diag(k0.astype(BF16), k1.astype(BF16))))
        lower = [jnp.where(tri_strict, t[:c_len] * d, 0.0) for t, d in zip(both, decay)]
        a_qk = [jnp.where(tri_incl, t[c_len:] * d, 0.0).astype(BF16) for t, d in zip(both, decay)]
        t_inv = [eye - lo * base_mask for lo in lower]
        for m in level_masks:
            t16 = [t.astype(BF16) for t in t_inv]
            left = [_dot(t, block_diag((lo * m).astype(BF16))).astype(BF16) for t, lo in zip(t16, lower)]
            t_inv = [t - _dot(l, block_diag(t_b)) for t, l, t_b in zip(t_inv, left, t16)]
        t16 = [t.astype(BF16) for t in t_inv]
        egc = [jnp.exp(t) for t in gcb]
        vb = [(vv * bb).astype(BF16) for vv, bb in zip(v, beta)]
        kbe = [(kk * e).astype(BF16) for kk, e in zip(kb, egc)]
        uw = []
        for (b, p), t in zip(pair_units, t16):
            vb0, vb1 = pair_of(vb, b, p)
            kbe0, kbe1 = pair_of(kbe, b, p)
            uw.append(_dot(t, stack_diag(jnp.concatenate([vb0, kbe0], axis=1), jnp.concatenate([vb1, kbe1], axis=1))))
        u, w = [], []
        for t in uw:
            for hh in range(2):
                u.append(t[:, 2 * hh * DN_DIM:(2 * hh + 1) * DN_DIM])
                w.append(t[:, (2 * hh + 1) * DN_DIM:(2 * hh + 2) * DN_DIM])
        k_tail = [(kk * jnp.exp(gl - gc)).astype(BF16) for kk, gl, gc in zip(k, g_last, gcb)]
        wq = [jnp.concatenate([ww, qq * e], axis=0).astype(BF16) for ww, qq, e in zip(w, q, egc)]
        s_decay = [jnp.exp(gl) for gl in g_last]
        return u, wq, a_qk, k_tail, s_decay

    def advance(j, prepared, state):
        n_units = len(seqs) * len(heads)
        u, wq, k_tail, s_decay = [t[j * n_units:(j + 1) * n_units] for t in (prepared[0], prepared[1], prepared[3], prepared[4])]
        a_qk = prepared[2][j * n_units // 2:(j + 1) * n_units // 2]
        ws = [_dot(t, s.astype(BF16)) for t, s in zip(wq, state)]
        vn16 = [(uu - t[:c_len]).astype(BF16) for uu, t in zip(u, ws)]
        new_state = [s * d + _dot_tn(kt, vn) for s, d, kt, vn in zip(state, s_decay, k_tail, vn16)]
        intra = [_dot(aq, stack_diag(vn16[2 * i], vn16[2 * i + 1])) for i, aq in enumerate(a_qk)]
        o = [t[c_len:] + intra[i // 2][:, (i % 2) * DN_DIM:(i % 2 + 1) * DN_DIM] for i, t in enumerate(ws)]
        o = [t * lax.rsqrt(jnp.mean(t * t, axis=-1, keepdims=True) + NORM_EPS) * out_gain for t in o]
        for i, t in enumerate(o):
            for r in range(SUBLANES):
                out_scr[i, pl.ds(j * c_len + r, groups, stride=SUBLANES), :] = t[r * groups:(r + 1) * groups]
        return new_state

    n_units = len(seqs) * len(heads)
    n_chunks = ts // c_len
    state = [st_ref[i] for i in range(n_units)]
    prepared = prepare(range(n_chunks))
    for j in range(n_chunks):
        state = advance(j, prepared, state)
    for i in range(n_units):
        st_ref[i] = state[i]
    for i in range(n_units):
        b, h = divmod(i, len(heads))
        lanes = slice(h * DN_DIM, (h + 1) * DN_DIM)
        o_ref[b, :, lanes] = (out_scr[i] * z_ref[b, :, lanes]).astype(o_ref.dtype)


def _deltanet(qkv, z_gate, ba, gate_params, out_gain, batch, ts=256):
    t = qkv.shape[0]
    seq = t // batch

    def rows(w):
        return pl.BlockSpec((batch, ts, w), lambda s: (0, s, 0))

    out = pl.pallas_call(
        functools.partial(_dn_kernel, ts=ts, batch=batch),
        grid=(seq // ts,),
        in_specs=[rows(3 * DN_WIDTH), rows(DN_WIDTH), rows(LANES),
                  pl.BlockSpec((8, LANES), lambda s: (0, 0)),
                  pl.BlockSpec((1, DN_DIM), lambda s: (0, 0))],
        out_specs=rows(DN_WIDTH),
        out_shape=jax.ShapeDtypeStruct((batch, seq, DN_WIDTH), BF16),
        scratch_shapes=[pltpu.VMEM((batch * DN_HEADS, DN_DIM, DN_DIM), F32),
                        pltpu.VMEM((batch * DN_HEADS, ts, DN_DIM), F32)],
        compiler_params=_params("arbitrary"),
        name="deltanet",
    )(qkv.reshape(batch, seq, -1), z_gate.reshape(batch, seq, -1), ba.reshape(batch, seq, -1), gate_params,
      out_gain.reshape(1, DN_DIM))
    return out.reshape(t, DN_WIDTH)


def _swa_bias(slopes, dilation):
    blk = SWA_BLOCK
    shape = (2 * blk, 2 * blk)
    row = lax.broadcasted_iota(jnp.int32, shape, 0)
    col = lax.broadcasted_iota(jnp.int32, shape, 1)
    rel = (row & (blk - 1)) + blk - col
    slope = jnp.where(row < blk, slopes[0] * dilation, slopes[1] * dilation)
    return jnp.where((rel >= 0) & (rel <= blk), -slope * rel.astype(F32), NEG_BIG)


def _swa_kernel(q0_ref, q1_ref, k0_ref, k1_ref, v0_ref, v1_ref, pk0_ref, pk1_ref, pv0_ref, pv1_ref,
                o_ref, acc_scr, lse_scr, *, tq, group):
    blk = SWA_BLOCK
    q_refs = (q0_ref, q1_ref)
    kv_refs = ((k0_ref, k1_ref), (v0_ref, v1_ref))
    prev_refs = ((pk0_ref, pk1_ref), (pv0_ref, pv1_ref))
    pairs = range(SWA_HEADS // 2)
    have_prev = pl.program_id(1) > 0

    low = lax.broadcasted_iota(jnp.int32, (blk, LANES), 1) < SWA_DIM
    same_block = lax.broadcasted_iota(jnp.int32, (2 * blk, 2 * blk), 1) >= blk
    slopes = [2.0 ** (-8.0 * (h + 1) / SWA_HEADS) for h in range(SWA_HEADS)]

    def rows_of(start, count, dilation):
        return pl.ds(start, count, stride=dilation) if dilation > 1 else pl.ds(start, count)

    def score_group(dilation, bias, starts, crosses_tile):
        units = range(len(starts))
        q_rows = [rows_of(s, blk, dilation) for s in starts]
        up = [(u, p) for u in units for p in pairs]
        q = [(q_refs[p][q_rows[u], :] * (SWA_DIM ** -0.5)).astype(BF16) for u, p in up]
        zero = jnp.zeros_like(q[0])
        q2 = [jnp.concatenate([jnp.where(low, t, zero), jnp.where(low, zero, t)], axis=0) for t in q]

        def keys_or_values(which, u, p):
            if crosses_tile[u]:
                before = prev_refs[which][p][rows_of(tq + starts[u] - dilation * blk, blk, dilation), :]
                return jnp.concatenate([before, kv_refs[which][p][q_rows[u], :]], axis=0).astype(BF16)
            return kv_refs[which][p][rows_of(starts[u] - dilation * blk, 2 * blk, dilation), :].astype(BF16)

        k2 = [keys_or_values(0, u, p) for u, p in up]
        v2 = [keys_or_values(1, u, p) for u, p in up]

        def unit_bias(u, p):
            if not crosses_tile[u]:
                return bias[p]
            return jnp.where(jnp.logical_or(same_block, have_prev), bias[p], NEG_BIG)

        s = [_dot_nt(a, b) + unit_bias(u, p) for (u, p), a, b in zip(up, q2, k2)]
        return up, q_rows, s, v2

    def finish_group(pi, scored):
        up, q_rows, s, v2 = scored
        m = [jnp.max(t, axis=-1, keepdims=True) for t in s]
        e = [jnp.exp(t - mm) for t, mm in zip(s, m)]
        den = [jnp.sum(t, axis=-1, keepdims=True) for t in e]
        pv = [_dot(t.astype(BF16), b) for t, b in zip(e, v2)]
        for (u, p), mm, dd, o in zip(up, m, den, pv):
            inv = 1.0 / dd
            lse = mm + jnp.log(dd)
            acc_scr[pi, p, q_rows[u], :] = jnp.where(low, o[:blk] * inv[:blk], o[blk:] * inv[blk:])
            lse_scr[pi, p, q_rows[u], :] = jnp.where(low, lse[:blk], lse[blk:])

    pending = None
    for pi, (_, dilation) in enumerate(SWA_PATTERNS):
        bias = [_swa_bias(slopes[2 * p:2 * p + 2], dilation) for p in pairs]
        per_residue = tq // (dilation * blk)
        units = [(r + dilation * blk * i, i == 0) for r in range(dilation) for i in range(per_residue)]
        for g0 in range(0, len(units), group):
            chunk = units[g0:g0 + group]
            scored = score_group(dilation, bias, [s for s, _ in chunk], [c for _, c in chunk])
            if pending is not None:
                finish_group(*pending)
            pending = (pi, scored)
    finish_group(*pending)

    merge_rows = 2 * SWA_BLOCK
    n_pat = len(SWA_PATTERNS)
    for c in range(tq // merge_rows):
        rows = slice(c * merge_rows, (c + 1) * merge_rows)
        for p in pairs:
            lses = [lse_scr[pi, p, rows, :] for pi in range(n_pat)]
            top = functools.reduce(jnp.maximum, lses)
            es = [jnp.exp(l - top) for l in lses]
            inv = 1.0 / functools.reduce(lambda a, b: a + b, es)
            merged = functools.reduce(lambda a, b: a + b, [es[pi] * acc_scr[pi, p, rows, :] for pi in range(n_pat)])
            o_ref[rows, p * LANES:(p + 1) * LANES] = (merged * inv).astype(o_ref.dtype)


def _swa(swa_qkv, batch, tq=2048, group=1):
    t = swa_qkv.shape[0]
    nj = t // batch // tq
    n_tiles = SWA_WIDTH // LANES
    n_pat = len(SWA_PATTERNS)

    def cur(c):
        return pl.BlockSpec((tq, LANES), lambda b, j: (b * nj + j, c))

    def prev(c):
        return pl.BlockSpec((tq, LANES), lambda b, j: (b * nj + jnp.maximum(j - 1, 0), c))

    cols = range(3 * n_tiles)
    kv_cols = range(n_tiles, 3 * n_tiles)
    return pl.pallas_call(
        functools.partial(_swa_kernel, tq=tq, group=group),
        grid=(batch, nj),
        in_specs=[cur(c) for c in cols] + [prev(c) for c in kv_cols],
        out_specs=pl.BlockSpec((tq, SWA_WIDTH), lambda b, j: (b * nj + j, 0)),
        out_shape=jax.ShapeDtypeStruct((t, SWA_WIDTH), BF16),
        scratch_shapes=[pltpu.VMEM((n_pat, n_tiles, tq, LANES), F32),
                        pltpu.VMEM((n_pat, n_tiles, tq, LANES), F32)],
        compiler_params=_params("parallel", "parallel"),
        name="swa",
    )(*([swa_qkv] * (len(cols) + len(kv_cols))))


def _kv_kernel(mem_ref, g_ref, w_ref, k_ref, v_ref):
    n = _rms(mem_ref[...], g_ref[...]).astype(BF16)
    d = k_ref.shape[-1]
    k_ref[...] = _dot(n, w_ref[:, :d]).astype(BF16)
    v_ref[...] = _dot(n, w_ref[:, d:]).astype(BF16)


def _mem_kv(mem, gain, w_kv, layer):
    b, m, d = mem.shape
    blk = pl.BlockSpec((None, m, d), lambda i: (i, 0, 0))
    return pl.pallas_call(
        _kv_kernel,
        grid=(b,),
        in_specs=[blk, pl.BlockSpec((1, d), lambda i: (0, 0)), _layer_spec(w_kv, layer)],
        out_specs=[blk, blk],
        out_shape=[jax.ShapeDtypeStruct((b, m, d), BF16)] * 2,
        compiler_params=_params("parallel"),
        name="xa_kv",
    )(mem, gain.reshape(1, d), w_kv)


def _mix_xa_kernel(h_ref, a_ref, b_ref, c_ref, wm_ref, g_ref, wq_ref, k_ref, v_ref, wo_ref, o_ref, att_scr):
    x = h_ref[...] + _dot(a_ref[...], wm_ref[0:DN_WIDTH, :])
    x = x + _dot(b_ref[...], wm_ref[DN_WIDTH:DN_WIDTH + SWA_WIDTH, :])
    x = x + _dot(c_ref[...], wm_ref[DN_WIDTH + SWA_WIDTH:, :])
    n = _rms(x, g_ref[...]).astype(BF16)
    d = x.shape[-1]
    dh = d // XA_HEADS
    q = _dot(n, wq_ref[...]).astype(BF16)
    lanes = [slice(h * dh, (h + 1) * dh) for h in range(XA_HEADS)]
    s = [_dot_nt(q[:, ln], k_ref[:, ln]) * (dh ** -0.5) for ln in lanes]
    m = [jnp.max(t, axis=-1, keepdims=True) for t in s]
    p = [jnp.exp(t - mm) for t, mm in zip(s, m)]
    p = [t * (1.0 / jnp.sum(t, axis=-1, keepdims=True)) for t in p]
    for t, ln in zip(p, lanes):
        att_scr[:, ln] = _dot(t.astype(BF16), v_ref[:, ln]).astype(BF16)
    o_ref[...] = x + _dot(att_scr[...], wo_ref[...])


def _mix_out_cross_attention(h, out_a, out_b, out_c, w_out, gain, w_q, k, v, w_o, layer, batch, tm=1024):
    t, d = h.shape
    ns = t // batch // tm
    m = k.shape[1]

    def row(w):
        return pl.BlockSpec((tm, w), lambda b, s: (b * ns + s, 0))

    kv = pl.BlockSpec((None, m, d), lambda b, s: (b, 0, 0))
    return pl.pallas_call(
        _mix_xa_kernel,
        grid=(batch, ns),
        in_specs=[row(d), row(DN_WIDTH), row(SWA_WIDTH), row(SG_WIDTH), _layer_spec(w_out, layer),
                  pl.BlockSpec((1, d), lambda b, s: (0, 0)), _layer_spec(w_q, layer), kv, kv, _layer_spec(w_o, layer)],
        out_specs=row(d),
        out_shape=jax.ShapeDtypeStruct((t, d), F32),
        scratch_shapes=[pltpu.VMEM((tm, d), BF16)],
        compiler_params=_params("parallel", "parallel"),
        name="mix_out_cross_attn",
    )(h, out_a, out_b, out_c, w_out, gain.reshape(1, d), w_q, k, v, w_o)


def kernel(x, mem, ffn1_norm, ffn1_w_gate_up, ffn1_w_down, mix_norm, mix_w_in, dn_conv_w, dn_a_log, dn_dt_bias,
           dn_out_norm, sg_norm_gain, sg_norm_bias, sg_w_spatial, sg_b_spatial, mix_w_out, xa_norm, xa_mem_norm,
           xa_w_q, xa_w_kv, xa_w_o, ffn2_norm, ffn2_w_gate_up, ffn2_w_down, final_norm):
    batch, seq, d = x.shape
    depth = ffn1_norm.shape[0]
    ffn1_gu, ffn1_d = _to_bf16(ffn1_w_gate_up), _to_bf16(ffn1_w_down)
    ffn2_gu, ffn2_d = _to_bf16(ffn2_w_gate_up), _to_bf16(ffn2_w_down)
    w_in, w_out = _mix_in_weights(mix_w_in), _to_bf16(mix_w_out)
    w_q, w_kv, w_o = _to_bf16(xa_w_q), _to_bf16(xa_w_kv), _to_bf16(xa_w_o)
    h = x.reshape(batch * seq, d)
    for i in range(depth):
        h = _ffn(h, ffn1_norm[i], ffn1_gu, ffn1_d, i)

        gate_params = jnp.zeros((8, LANES), F32)
        gate_params = gate_params.at[0, DN_HEADS:2 * DN_HEADS].set(dn_a_log[i])
        gate_params = gate_params.at[1, DN_HEADS:2 * DN_HEADS].set(dn_dt_bias[i])
        b_sp_rows = jnp.repeat(sg_b_spatial[i].T, SG_DIM, axis=1)
        qkv, z_gate, ba, swa_qkv, out_c = _proj(h, mix_norm[i], w_in, i, dn_conv_w[i], sg_norm_gain[i],
                                                sg_norm_bias[i], sg_w_spatial[i], b_sp_rows, seq)
        out_a = _deltanet(qkv, z_gate, ba, gate_params, dn_out_norm[i], batch)
        out_b = _swa(swa_qkv, batch)
        k, v = _mem_kv(mem, xa_mem_norm[i], w_kv, i)
        h = _mix_out_cross_attention(h, out_a, out_b, out_c, w_out, xa_norm[i], w_q, k, v, w_o, i, batch)
        h = _ffn(h, ffn2_norm[i], ffn2_gu, ffn2_d, i, final_gain=final_norm if i == depth - 1 else None)
    return h.reshape(batch, seq, d)
```
